```python
import math
import jax, jax.numpy as jnp
from jax import lax
import numpy as np

D_MODEL = 1024
BATCH = 2
SEQ = 16384
DEPTH = 1
DEC_BATCH = 128
DEC_SEQ = 8
PAST_LEN = 8192
PAGE_SIZE = 128

D_MIX = D_MODEL
HEAD_DIM = 64
D_ATTN = D_MIX // 2
N_HEADS_A = D_ATTN // HEAD_DIM
D_LRU = D_MIX - D_ATTN
N_HEADS_LRU = 8
LRU_BLOCK = D_LRU // N_HEADS_LRU
LRU_CONV_W = 4
LRU_C = 8.0
N_IDX_HEADS = 8
D_IDX = 64
INDEX_SCALE = (N_IDX_HEADS * D_IDX) ** -0.5
TOPK_MAX = 256
Q_BLOCK = 128
N_BUCKETS = 32
MAX_DISTANCE = 128
D_FF = 3 * D_MODEL
FFN_CONV_W = 3
EPS = 1e-6

SPLIT_POINTS = (
    D_ATTN,
    2 * D_ATTN,
    3 * D_ATTN,
    3 * D_ATTN + N_IDX_HEADS * D_IDX,
    3 * D_ATTN + N_IDX_HEADS * D_IDX + D_IDX,
    3 * D_ATTN + N_IDX_HEADS * D_IDX + D_IDX + N_IDX_HEADS,
    3 * D_ATTN + N_IDX_HEADS * D_IDX + D_IDX + N_IDX_HEADS + D_LRU,
)
D_IN = 3 * D_ATTN + N_IDX_HEADS * D_IDX + D_IDX + N_IDX_HEADS + 2 * D_LRU

kernel_name = 'sparse_index_rglru_hybrid_step'


def rmsnorm(x, g):
    xf = x.astype(jnp.float32)
    y = xf * lax.rsqrt(jnp.mean(xf * xf, axis=-1, keepdims=True) + EPS)
    return (y * g.astype(jnp.float32)).astype(x.dtype)


def t5_bucket(dist):
    n = jnp.maximum(dist, 0)
    max_exact = N_BUCKETS // 2
    nf = jnp.maximum(n, 1).astype(jnp.float32)
    large = max_exact + (jnp.log(nf / max_exact) / math.log(MAX_DISTANCE / max_exact)
                         * (N_BUCKETS - max_exact)).astype(jnp.int32)
    large = jnp.minimum(large, N_BUCKETS - 1)
    return jnp.where(n < max_exact, n, large)


def causal_dwconv(buf, x, w, b):
    width = w.shape[0]
    T = x.shape[1]
    xc = jnp.concatenate([buf.astype(x.dtype), x], axis=1)
    out = xc[:, 0:T] * w[0] + b
    for j in range(1, width):
        out = out + xc[:, j:j + T] * w[j]
    return out, xc[:, T:]


def indexer_scores(q_idx, w_idx, k_idx):
    dots = jnp.einsum('bqhd,bld->bqhl', q_idx.astype(jnp.float32), k_idx.astype(jnp.float32))
    return jnp.einsum('bqhl,bqh->bql', jax.nn.relu(dots), w_idx.astype(jnp.float32)) * INDEX_SCALE


def select_topk(scores, q_pos, k_sel):
    L = scores.shape[-1]
    kpos = jnp.arange(L, dtype=jnp.int32)
    masked = jnp.where(kpos[None, None, :] <= q_pos[None, :, None], scores, -jnp.inf)
    _, sel = lax.top_k(masked, k_sel)
    return sel.astype(jnp.int32)


def sparse_attend(q, q_pos, sel, k_sel, v_sel, rel_bias):
    logits = jnp.einsum('bqhd,bqkhd->bhqk', q, k_sel).astype(jnp.float32) * (HEAD_DIM ** -0.5)
    bucket = t5_bucket(q_pos[None, :, None] - sel)
    logits = logits + rel_bias.astype(jnp.float32)[bucket].transpose(0, 3, 1, 2)
    valid = (sel <= q_pos[None, :, None])[:, None]
    p = jax.nn.softmax(jnp.where(valid, logits, -jnp.inf), axis=-1)
    return jnp.einsum('bhqk,bqkhd->bqhd', p.astype(v_sel.dtype), v_sel)


def attn_prompt(q, k, v, q_idx, w_idx, k_idx, rel_bias):
    B, S = q.shape[0], q.shape[1]
    k_sel = min(TOPK_MAX, S // 4)
    n_blocks = S // Q_BLOCK

    def block(i):
        t0 = i * Q_BLOCK
        qb = lax.dynamic_slice_in_dim(q, t0, Q_BLOCK, axis=1)
        qib = lax.dynamic_slice_in_dim(q_idx, t0, Q_BLOCK, axis=1)
        wib = lax.dynamic_slice_in_dim(w_idx, t0, Q_BLOCK, axis=1)
        pos = t0 + jnp.arange(Q_BLOCK, dtype=jnp.int32)
        sel = select_topk(indexer_scores(qib, wib, k_idx), pos, k_sel)
        ksel = jax.vmap(lambda kb, ib: kb[ib])(k, sel)
        vsel = jax.vmap(lambda vb, ib: vb[ib])(v, sel)
        return sparse_attend(qb, pos, sel, ksel, vsel, rel_bias)

    out = lax.map(block, jnp.arange(n_blocks, dtype=jnp.int32))
    return out.transpose(1, 0, 2, 3, 4).reshape(B, S, N_HEADS_A, HEAD_DIM)


def attn_sample(l, q, k_new, v_new, q_idx, w_idx, k_idx_new, cache_k, cache_v, cache_idx_k,
                page_table, rel_bias):
    Bd, T = q.shape[0], q.shape[1]
    n_pages = page_table.shape[1]
    past = n_pages * PAGE_SIZE
    k_sel = min(TOPK_MAX, (past + T) // 4)
    past_idx = cache_idx_k[l, page_table].reshape(Bd, past, D_IDX).astype(k_idx_new.dtype)
    keys_idx = jnp.concatenate([past_idx, k_idx_new], axis=1)
    pos = past + jnp.arange(T, dtype=jnp.int32)
    sel = select_topk(indexer_scores(q_idx, w_idx, keys_idx), pos, k_sel)
    is_past = (sel < past)[..., None, None]
    ps = jnp.minimum(sel, past - 1)
    phys = jnp.take_along_axis(page_table, (ps // PAGE_SIZE).reshape(Bd, -1), axis=1).reshape(sel.shape)
    slot = ps % PAGE_SIZE
    ns = jnp.clip(sel - past, 0, T - 1)
    ksel = jnp.where(is_past, cache_k[l, phys, slot].astype(q.dtype),
                     jax.vmap(lambda kb, ib: kb[ib])(k_new, ns))
    vsel = jnp.where(is_past, cache_v[l, phys, slot].astype(q.dtype),
                     jax.vmap(lambda vb, ib: vb[ib])(v_new, ns))
    return sparse_attend(q, pos, sel, ksel, vsel, rel_bias)


def rglru_branch(x_in, gate_in, conv_buf, h0, conv_w, conv_b, w_a, b_a, w_x, b_x, lam):
    xc, new_buf = causal_dwconv(conv_buf, x_in, conv_w, conv_b)
    B, T = xc.shape[0], xc.shape[1]
    xb = xc.reshape(B, T, N_HEADS_LRU, LRU_BLOCK)
    r = jax.nn.sigmoid((jnp.einsum('bthi,hij->bthj', xb, w_a).reshape(B, T, D_LRU) + b_a).astype(jnp.float32))
    ig = jax.nn.sigmoid((jnp.einsum('bthi,hij->bthj', xb, w_x).reshape(B, T, D_LRU) + b_x).astype(jnp.float32))
    log_a = -LRU_C * r * jax.nn.softplus(-lam.astype(jnp.float32))
    a = jnp.exp(log_a)
    mult = jnp.sqrt(-jnp.expm1(2.0 * log_a))
    bterm = mult * ig * xc.astype(jnp.float32)
    bterm = bterm.at[:, 0].add(a[:, 0] * h0.astype(jnp.float32))

    def combine(e1, e2):
        a1, b1 = e1
        a2, b2 = e2
        return a1 * a2, a2 * b1 + b2

    _, h = lax.associative_scan(combine, (a, bterm), axis=1)
    y = h.astype(x_in.dtype) * jax.nn.gelu(gate_in)
    return y, new_buf, h[:, -1].astype(x_in.dtype)


def conv_ffn(xn, buf, w_up, conv_w, conv_b, w_down):
    g, u = jnp.split(xn @ w_up, 2, axis=-1)
    gc, new_buf = causal_dwconv(buf, g, conv_w, conv_b)
    return (jax.nn.gelu(gc) * u) @ w_down, new_buf


def combined_projection(x, norm_g, w_in_l):
    B, T = x.shape[0], x.shape[1]
    z = rmsnorm(x, norm_g) @ w_in_l
    q, k, v, qi, ki, wi, lx, lg = jnp.split(z, SPLIT_POINTS, axis=-1)
    q = q.reshape(B, T, N_HEADS_A, HEAD_DIM)
    k = k.reshape(B, T, N_HEADS_A, HEAD_DIM)
    v = v.reshape(B, T, N_HEADS_A, HEAD_DIM)
    qi = qi.reshape(B, T, N_IDX_HEADS, D_IDX)
    return q, k, v, qi, ki, wi, lx, lg


def merge_and_ffn(x, attn_out, lru_out, ffn_buf, w_out_l, norm_ffn_l, w_up, conv_w, conv_b, w_down):
    B, T = x.shape[0], x.shape[1]
    x = x + jnp.concatenate([attn_out.reshape(B, T, D_ATTN), lru_out], axis=-1) @ w_out_l
    f, new_buf = conv_ffn(rmsnorm(x, norm_ffn_l), ffn_buf, w_up, conv_w, conv_b, w_down)
    return x + f, new_buf


def setup_inputs(seed: int = 0) -> dict:
    key = jax.random.key(seed)
    ks = jax.random.split(key, 28)
    n_pages = PAST_LEN // PAGE_SIZE
    n_phys = (DEC_BATCH * n_pages * 5) // 4

    def nrm(k, shape, scale):
        return jax.random.normal(k, shape, jnp.float32) * scale

    a0 = jax.random.uniform(ks[21], (DEPTH, D_LRU), jnp.float32, 0.9, 0.999)
    s0 = a0 ** (1.0 / LRU_C)
    return {
        'x_prompt': nrm(ks[0], (BATCH, SEQ, D_MODEL), 1.0),
        'x_sample': nrm(ks[1], (DEC_BATCH, DEC_SEQ, D_MODEL), 1.0),
        'cache_k': nrm(ks[2], (DEPTH, n_phys, PAGE_SIZE, N_HEADS_A, HEAD_DIM), 1.0),
        'cache_v': nrm(ks[3], (DEPTH, n_phys, PAGE_SIZE, N_HEADS_A, HEAD_DIM), 1.0),
        'cache_idx_k': nrm(ks[4], (DEPTH, n_phys, PAGE_SIZE, D_IDX), 1.0),
        'state_lru_h': nrm(ks[5], (DEPTH, DEC_BATCH, D_LRU), 0.5),
        'state_lru_conv': nrm(ks[6], (DEPTH, DEC_BATCH, LRU_CONV_W - 1, D_LRU), 1.0),
        'state_ffn_conv': nrm(ks[7], (DEPTH, DEC_BATCH, FFN_CONV_W - 1, D_FF), 1.0),
        'page_table': jax.random.permutation(ks[8], n_phys)[: DEC_BATCH * n_pages]
                      .reshape(DEC_BATCH, n_pages).astype(jnp.int32),
        'w_in': nrm(ks[9], (DEPTH, D_MODEL, D_IN), D_MODEL ** -0.5),
        'w_out': nrm(ks[10], (DEPTH, D_MIX, D_MODEL), D_MIX ** -0.5),
        'norm_mix': 1.0 + nrm(ks[11], (DEPTH, D_MODEL), 0.02),
        'norm_ffn': 1.0 + nrm(ks[12], (DEPTH, D_MODEL), 0.02),
        'norm_final': 1.0 + nrm(ks[13], (D_MODEL,), 0.02),
        'rel_bias': nrm(ks[14], (N_BUCKETS, N_HEADS_A), 0.5),
        'lru_conv_w': nrm(ks[15], (DEPTH, LRU_CONV_W, D_LRU), LRU_CONV_W ** -0.5),
        'lru_conv_b': nrm(ks[16], (DEPTH, D_LRU), 0.01),
        'lru_w_a': nrm(ks[17], (DEPTH, N_HEADS_LRU, LRU_BLOCK, LRU_BLOCK), LRU_BLOCK ** -0.5),
        'lru_b_a': nrm(ks[18], (DEPTH, D_LRU), 0.01),
        'lru_w_x': nrm(ks[19], (DEPTH, N_HEADS_LRU, LRU_BLOCK, LRU_BLOCK), LRU_BLOCK ** -0.5),
        'lru_b_x': nrm(ks[20], (DEPTH, D_LRU), 0.01),
        'lru_lambda': jnp.log(s0) - jnp.log1p(-s0),
        'ffn_w_up': nrm(ks[22], (DEPTH, D_MODEL, 2 * D_FF), D_MODEL ** -0.5),
        'ffn_conv_w': nrm(ks[23], (DEPTH, FFN_CONV_W, D_FF), FFN_CONV_W ** -0.5),
        'ffn_conv_b': nrm(ks[24], (DEPTH, D_FF), 0.01),
        'ffn_w_down': nrm(ks[25], (DEPTH, D_FF, D_MODEL), D_FF ** -0.5),
    }


def reference(x_prompt, x_sample, cache_k, cache_v, cache_idx_k, state_lru_h, state_lru_conv,
              state_ffn_conv, page_table, w_in, w_out, norm_mix, norm_ffn, norm_final, rel_bias,
              lru_conv_w, lru_conv_b, lru_w_a, lru_b_a, lru_w_x, lru_b_x, lru_lambda,
              ffn_w_up, ffn_conv_w, ffn_conv_b, ffn_w_down):
    xp, xs = x_prompt, x_sample
    Bp = xp.shape[0]
    kp_l, vp_l, ip_l, hp_l, lcp_l, fcp_l = [], [], [], [], [], []
    ks_l, vs_l, is_l, hs_l, lcs_l, fcs_l = [], [], [], [], [], []
    for l in range(DEPTH):
        lru_p = (lru_conv_w[l], lru_conv_b[l], lru_w_a[l], lru_b_a[l], lru_w_x[l], lru_b_x[l], lru_lambda[l])
        ffn_p = (ffn_w_up[l], ffn_conv_w[l], ffn_conv_b[l], ffn_w_down[l])

        q, k, v, qi, ki, wi, lx, lg = combined_projection(xp, norm_mix[l], w_in[l])
        a_out = attn_prompt(q, k, v, qi, wi, ki, rel_bias)
        r_out, lconv, h_last = rglru_branch(
            lx, lg, jnp.zeros((Bp, LRU_CONV_W - 1, D_LRU), xp.dtype), jnp.zeros((Bp, D_LRU), xp.dtype), *lru_p)
        xp, fconv = merge_and_ffn(
            xp, a_out, r_out, jnp.zeros((Bp, FFN_CONV_W - 1, D_FF), xp.dtype), w_out[l], norm_ffn[l], *ffn_p)
        kp_l.append(k); vp_l.append(v); ip_l.append(ki)
        hp_l.append(h_last); lcp_l.append(lconv); fcp_l.append(fconv)

        q, k, v, qi, ki, wi, lx, lg = combined_projection(xs, norm_mix[l], w_in[l])
        a_out = attn_sample(l, q, k, v, qi, wi, ki, cache_k, cache_v, cache_idx_k, page_table, rel_bias)
        r_out, lconv, h_last = rglru_branch(lx, lg, state_lru_conv[l], state_lru_h[l], *lru_p)
        xs, fconv = merge_and_ffn(xs, a_out, r_out, state_ffn_conv[l], w_out[l], norm_ffn[l], *ffn_p)
        ks_l.append(k); vs_l.append(v); is_l.append(ki)
        hs_l.append(h_last); lcs_l.append(lconv); fcs_l.append(fconv)

    y_prompt = rmsnorm(xp, norm_final)
    y_sample = rmsnorm(xs, norm_final)
    return (y_prompt, y_sample,
            jnp.stack(kp_l), jnp.stack(vp_l), jnp.stack(ip_l),
            jnp.stack(hp_l), jnp.stack(lcp_l), jnp.stack(fcp_l),
            jnp.stack(ks_l), jnp.stack(vs_l), jnp.stack(is_l),
            jnp.stack(hs_l), jnp.stack(lcs_l), jnp.stack(fcs_l))
```

```python
import functools
import math

import jax
import jax.numpy as jnp
import numpy as np
from jax import lax
from jax.experimental import pallas as pl
from jax.experimental.pallas import tpu as pltpu

F32 = jnp.float32
BF16 = jnp.bfloat16
I32 = jnp.int32

HEAD_DIM = 64
N_HEADS_A = 8
D_ATTN = HEAD_DIM * N_HEADS_A
N_IDX_HEADS = 8
D_IDX = 64
D_QI = N_IDX_HEADS * D_IDX
INDEX_SCALE = (N_IDX_HEADS * D_IDX) ** -0.5
TOPK_MAX = 256
N_HEADS_LRU = 8
LRU_CONV_W = 4
LRU_C = 8.0
N_BUCKETS = 32
MAX_DISTANCE = 128
FFN_CONV_W = 3
EPS = 1e-6
PAGE_SIZE = 128

FAR_DISTANCE = 113
NEG = -1e30
NO_TIE_CUT = float(2 ** 30)

VMEM_LIMIT_BYTES = 56 * 1024 * 1024


def _params(sem):
    return pltpu.CompilerParams(dimension_semantics=sem, vmem_limit_bytes=VMEM_LIMIT_BYTES)


def _rmsnorm(x, g):
    ms = jnp.mean(x * x, axis=-1, keepdims=True)
    return (x * lax.rsqrt(ms + EPS)) * g


def _inproj_kernel(x_ref, g_ref, w_ref, qb_ref, k_ref, v_ref, kb_ref, vb_ref, qib_ref, ki_ref,
                   kib_ref, wi_ref, lx_ref, lg_ref, *, d_lru):
    xn = _rmsnorm(x_ref[...], g_ref[...])
    z = jnp.dot(xn.astype(BF16), w_ref[...], preferred_element_type=F32)
    o = 0
    q = z[:, o:o + D_ATTN]; o += D_ATTN
    k = z[:, o:o + D_ATTN]; o += D_ATTN
    v = z[:, o:o + D_ATTN]; o += D_ATTN
    qi = z[:, o:o + D_QI]; o += D_QI
    lx = z[:, o:o + d_lru]; o += d_lru
    lg = z[:, o:o + d_lru]; o += d_lru
    ki = z[:, o:o + D_IDX]; o += D_IDX
    wi = z[:, o:o + N_IDX_HEADS]
    qb_ref[...] = (q * (HEAD_DIM ** -0.5)).astype(BF16)
    k_ref[...] = k
    v_ref[...] = v
    kb_ref[...] = k.astype(BF16)
    vb_ref[...] = v.astype(BF16)
    qib_ref[...] = qi.astype(BF16)
    ki_ref[...] = ki
    kib_ref[...] = ki.astype(BF16)
    wi_ref[...] = wi
    lx_ref[...] = lx
    lg_ref[...] = lg


def _in_projection(x2d, norm_g, w_re, d_lru, tm):
    n, d = x2d.shape
    nc = w_re.shape[1]
    row = lambda w: pl.BlockSpec((tm, w), lambda i: (i, 0))
    widths = [(D_ATTN, BF16), (D_ATTN, F32), (D_ATTN, F32), (D_ATTN, BF16), (D_ATTN, BF16),
              (D_QI, BF16), (D_IDX, F32), (D_IDX, BF16), (N_IDX_HEADS, F32), (d_lru, F32), (d_lru, F32)]
    return pl.pallas_call(
        functools.partial(_inproj_kernel, d_lru=d_lru),
        grid=(n // tm,),
        in_specs=[row(d), pl.BlockSpec((1, d), lambda i: (0, 0)),
                  pl.BlockSpec((d, nc), lambda i: (0, 0))],
        out_specs=[row(w) for w, _ in widths],
        out_shape=[jax.ShapeDtypeStruct((n, w), dt) for w, dt in widths],
        compiler_params=_params(("parallel",)),
        name="in_projection",
    )(x2d, norm_g.reshape(1, d), w_re)


def _softplus(x):
    return jnp.maximum(x, 0.0) + jnp.log1p(jnp.exp(-jnp.abs(x)))


def _lru_kernel(lx_ref, lg_ref, buf_ref, h0_ref, cw_ref, cb_ref, wax_ref, bax_ref, lam_ref,
                y_ref, newbuf_ref, hlast_ref, xbuf, hcar, a_s, h_s, *, tc, d_lru):
    c = pl.program_id(1)

    @pl.when(c == 0)
    def _():
        xbuf[0:8, :] = buf_ref[0]
        hcar[...] = h0_ref[0]

    xbuf[8:8 + tc, :] = lx_ref[0]
    xc = xbuf[5:5 + tc, :] * cw_ref[0:1, :] + cb_ref[...]
    for j in range(1, LRU_CONV_W):
        xc = xc + xbuf[5 + j:5 + j + tc, :] * cw_ref[j:j + 1, :]
    gates = jnp.dot(xc.astype(BF16), wax_ref[...], preferred_element_type=F32) + bax_ref[...]
    r = jax.nn.sigmoid(gates[:, :d_lru])
    ig = jax.nn.sigmoid(gates[:, d_lru:])
    log_a = -LRU_C * r * _softplus(-lam_ref[...])
    a_s[...] = jnp.exp(log_a)
    neg_expm1 = -jnp.tanh(log_a) * (jnp.exp(2.0 * log_a) + 1.0)
    h_s[...] = jnp.sqrt(neg_expm1) * ig * xc

    def group(gi, h):
        r0 = pl.multiple_of(gi * 8, 8)
        a8 = a_s[pl.ds(r0, 8), :]
        b8 = h_s[pl.ds(r0, 8), :]
        rows = []
        for t in range(8):
            h = a8[t:t + 1, :] * h + b8[t:t + 1, :]
            rows.append(h)
        h_s[pl.ds(r0, 8), :] = jnp.concatenate(rows, axis=0)
        return h

    h = lax.fori_loop(0, tc // 8, group, hcar[...])
    hcar[...] = h
    y_ref[0] = (h_s[...] * jax.nn.gelu(lg_ref[0])).astype(y_ref.dtype)
    newbuf_ref[0] = xbuf[tc + 5:tc + 8, :]
    hlast_ref[0] = h
    xbuf[0:8, :] = xbuf[tc:tc + 8, :]


def _lru_branch(lx, lg, buf8, h0, conv_w, conv_b, wax, bax, lam, tc):
    b, t, c = lx.shape
    seq = lambda: pl.BlockSpec((1, tc, c), lambda i, j: (i, j, 0))
    per_b = lambda r: pl.BlockSpec((1, r, c), lambda i, j: (i, 0, 0))
    const = lambda shp: pl.BlockSpec(shp, lambda i, j: (0, 0))
    return pl.pallas_call(
        functools.partial(_lru_kernel, tc=tc, d_lru=c),
        grid=(b, t // tc),
        in_specs=[seq(), seq(), per_b(8), per_b(1), const((LRU_CONV_W, c)), const((1, c)),
                  const((c, 2 * c)), const((1, 2 * c)), const((1, c))],
        out_specs=[seq(), per_b(LRU_CONV_W - 1), per_b(1)],
        out_shape=[jax.ShapeDtypeStruct((b, t, c), BF16),
                   jax.ShapeDtypeStruct((b, LRU_CONV_W - 1, c), F32),
                   jax.ShapeDtypeStruct((b, 1, c), F32)],
        scratch_shapes=[pltpu.VMEM((tc + 8, c), F32), pltpu.VMEM((1, c), F32),
                        pltpu.VMEM((tc, c), F32), pltpu.VMEM((tc, c), F32)],
        compiler_params=_params(("parallel", "arbitrary")),
        name="rglru_branch",
    )(lx, lg, buf8, h0, conv_w, conv_b.reshape(1, c), wax, bax.reshape(1, 2 * c), lam.reshape(1, c))


def _t5_bucket(dist):
    n = jnp.maximum(dist, 0)
    max_exact = N_BUCKETS // 2
    nf = jnp.maximum(n, 1).astype(F32)
    large = max_exact + (jnp.log(nf / max_exact) / math.log(MAX_DISTANCE / max_exact)
                         * (N_BUCKETS - max_exact)).astype(I32)
    large = jnp.minimum(large, N_BUCKETS - 1)
    return jnp.where(n < max_exact, n, large)


def _bias_lookup(rb_ref, bucket, head):
    acc = jnp.zeros(bucket.shape, F32)
    for b in range(N_BUCKETS):
        acc = jnp.where(bucket == b, rb_ref[b * N_HEADS_A + head], acc)
    return acc


def _bias_tiles_kernel(rb_ref, o_ref, *, tk, tq, d0_min):
    kind = pl.program_id(0)
    head = pl.program_id(1)
    ss = lax.broadcasted_iota(I32, (tk, tq), 0)
    tt = lax.broadcasted_iota(I32, (tk, tq), 1)
    d = d0_min + kind * tk + tt - ss
    val = _bias_lookup(rb_ref, _t5_bucket(d), head)
    o_ref[0, 0] = jnp.where(d < 0, NEG, val)


def _prompt_bias_tiles(rel_bias, tk, tq):
    d0_min = tk - tq
    n_kind = -(-(tk - 1 + FAR_DISTANCE - d0_min) // tk) + 1
    tiles = pl.pallas_call(
        functools.partial(_bias_tiles_kernel, tk=tk, tq=tq, d0_min=d0_min),
        grid=(n_kind, N_HEADS_A),
        in_specs=[pl.BlockSpec(memory_space=pltpu.SMEM)],
        out_specs=pl.BlockSpec((1, 1, tk, tq), lambda k, h: (k, h, 0, 0)),
        out_shape=jax.ShapeDtypeStruct((n_kind, N_HEADS_A, tk, tq), F32),
        compiler_params=_params(("parallel", "parallel")),
        name="prompt_bias_tiles",
    )(rel_bias.reshape(-1))
    return tiles, n_kind, d0_min


def _sample_bias_kernel(rb_ref, o_ref, *, t_new):
    kind = pl.program_id(0)
    rows = N_HEADS_A * t_new
    r = lax.broadcasted_iota(I32, (rows, PAGE_SIZE), 0)
    ss = lax.broadcasted_iota(I32, (rows, PAGE_SIZE), 1)
    tq = r % t_new
    head = r // t_new
    base = jnp.where(kind == 0, FAR_DISTANCE + PAGE_SIZE, jnp.where(kind == 1, PAGE_SIZE, 0))
    d = base + tq - ss
    bucket = _t5_bucket(d)
    acc = jnp.zeros((rows, PAGE_SIZE), F32)
    for h in range(N_HEADS_A):
        acc = jnp.where(head == h, _bias_lookup(rb_ref, bucket, h), acc)
    o_ref[0] = jnp.where(d < 0, NEG, acc)


def _sample_bias_tiles(rel_bias, t_new):
    rows = N_HEADS_A * t_new
    return pl.pallas_call(
        functools.partial(_sample_bias_kernel, t_new=t_new),
        grid=(3,),
        in_specs=[pl.BlockSpec(memory_space=pltpu.SMEM)],
        out_specs=pl.BlockSpec((1, rows, PAGE_SIZE), lambda k: (k, 0, 0)),
        out_shape=jax.ShapeDtypeStruct((3, rows, PAGE_SIZE), F32),
        compiler_params=_params(("parallel",)),
        name="sample_bias_tiles",
    )(rel_bias.reshape(-1))


def _tri_tables(n_q_blocks, tq, tk):
    ii, jj = [], []
    for i in range(n_q_blocks):
        for j in range((i + 1) * tq // tk):
            ii.append(i)
            jj.append(j)
    return np.asarray(ii, np.int32), np.asarray(jj, np.int32)


def _scores_kernel(ii_ref, jj_ref, ki_ref, qit_ref, wit_ref, o_ref, *, tk, tq):
    s = pl.program_id(1)
    i = ii_ref[s]
    j = jj_ref[s]
    ki = ki_ref[0]
    acc = jnp.zeros((tk, tq), F32)
    for h in range(N_IDX_HEADS):
        d = jnp.dot(ki, qit_ref[0, h * D_IDX:(h + 1) * D_IDX, :], preferred_element_type=F32)
        acc = acc + jnp.maximum(d, 0.0) * wit_ref[0, h:h + 1, :]
    sc = acc * INDEX_SCALE
    kpos = j * tk + lax.broadcasted_iota(I32, (tk, tq), 0)
    qpos = i * tq + lax.broadcasted_iota(I32, (tk, tq), 1)
    o_ref[0] = jnp.where(kpos <= qpos, sc, -jnp.inf)


def _prompt_scores(kib, qit, wit, tk, tq):
    b, s, _ = kib.shape
    ii, jj = _tri_tables(s // tq, tq, tk)
    grid_spec = pltpu.PrefetchScalarGridSpec(
        num_scalar_prefetch=2,
        grid=(b, len(ii)),
        in_specs=[pl.BlockSpec((1, tk, D_IDX), lambda bb, st, ii, jj: (bb, jj[st], 0)),
                  pl.BlockSpec((1, D_QI, tq), lambda bb, st, ii, jj: (bb, 0, ii[st])),
                  pl.BlockSpec((1, N_IDX_HEADS, tq), lambda bb, st, ii, jj: (bb, 0, ii[st]))],
        out_specs=pl.BlockSpec((1, tk, tq), lambda bb, st, ii, jj: (bb, jj[st], ii[st])),
    )
    return pl.pallas_call(
        functools.partial(_scores_kernel, tk=tk, tq=tq),
        grid_spec=grid_spec,
        out_shape=jax.ShapeDtypeStruct((b, s, s), F32),
        compiler_params=_params(("parallel", "arbitrary")),
        name="prompt_indexer_scores",
    )(jnp.asarray(ii), jnp.asarray(jj), kib, qit, wit)


def _to_key(x):
    bits = lax.bitcast_convert_type(x, I32)
    key = bits ^ ((bits >> 31) & 0x7FFFFFFF)
    return jnp.where(x == 0.0, 0, key)


def _from_key(key):
    bits = key ^ ((key >> 31) & 0x7FFFFFFF)
    return lax.bitcast_convert_type(bits, F32)


def _threshold_kernel(ii_ref, jj_ref, nj_ref, st_ref, o_ref, slab, *, tk, tq, k_sel, rows):
    s = pl.program_id(1)
    j = jj_ref[s]
    nj = nj_ref[s]
    slab[j] = _to_key(st_ref[0])

    def count(pred):
        def body(jb, acc):
            for rc in range(tk // rows):
                acc = acc + jnp.where(pred(slab[jb, rc * rows:(rc + 1) * rows, :], jb, rc), 1, 0)
            return acc
        acc = lax.fori_loop(0, nj, body, jnp.zeros((rows, tq), I32))
        return jnp.sum(acc, axis=0, keepdims=True)

    @pl.when(j == nj - 1)
    def _():
        int_min = jnp.int32(-2 ** 31)
        c0 = count(lambda blk, jb, rc: blk >= 0)
        kth = jnp.where(c0 >= k_sel, 0, int_min).astype(I32)

        def bit_step(it, kth):
            cand = kth | (jnp.int32(1) << (30 - it))
            c = count(lambda blk, jb, rc: blk >= cand)
            return jnp.where(c >= k_sel, cand, kth)

        kth = lax.fori_loop(0, 31, bit_step, kth)
        n_gt = count(lambda blk, jb, rc: blk > kth)
        n_eq = count(lambda blk, jb, rc: blk == kth)
        need = k_sel - n_gt
        neg_inf_key = _to_key(jnp.full((1, tq), -jnp.inf, F32))
        tie_cut = (n_eq > need) & (kth != neg_inf_key)
        o_ref[0, 0:1, :] = _from_key(kth)
        o_ref[0, 1:2, :] = jnp.full((1, tq), NO_TIE_CUT, F32)
        o_ref[0, 2:8, :] = jnp.zeros((6, tq), F32)

        @pl.when(jnp.max(tie_cut.astype(I32)) > 0)
        def _():
            row = lax.broadcasted_iota(I32, (rows, tq), 0)

            def pos_step(it, cut):
                cand = cut | (jnp.int32(1) << (23 - it))
                c = count(lambda blk, jb, rc: (blk == kth) & (jb * tk + rc * rows + row < cand))
                return jnp.where(c < need, cand, cut)

            cut = lax.fori_loop(0, 24, pos_step, jnp.zeros((1, tq), I32))
            o_ref[0, 1:2, :] = jnp.where(tie_cut, cut.astype(F32), NO_TIE_CUT)


def _topk_threshold(st, tk, tq, k_sel, causal):
    b, l, n = st.shape
    if causal:
        ii, jj = _tri_tables(n // tq, tq, tk)
    else:
        nb = l // tk
        ii = np.repeat(np.arange(n // tq, dtype=np.int32), nb)
        jj = np.tile(np.arange(nb, dtype=np.int32), n // tq)
    nj = np.asarray([(i + 1) * tq // tk if causal else l // tk for i in ii], np.int32)
    rows = min(tk, 64)
    grid_spec = pltpu.PrefetchScalarGridSpec(
        num_scalar_prefetch=3,
        grid=(b, len(ii)),
        in_specs=[pl.BlockSpec((1, tk, tq), lambda bb, st_, ii, jj, nj: (bb, jj[st_], ii[st_]))],
        out_specs=pl.BlockSpec((1, 8, tq), lambda bb, st_, ii, jj, nj: (bb, 0, ii[st_])),
        scratch_shapes=[pltpu.VMEM((int(nj.max()), tk, tq), I32)],
    )
    return pl.pallas_call(
        functools.partial(_threshold_kernel, tk=tk, tq=tq, k_sel=k_sel, rows=rows),
        grid_spec=grid_spec,
        out_shape=jax.ShapeDtypeStruct((b, 8, n), F32),
        compiler_params=_params(("parallel", "arbitrary")),
        name="topk_threshold",
    )(jnp.asarray(ii), jnp.asarray(jj), jnp.asarray(nj), st)


def _selected(score, key_pos, thr, cut):
    return (score > thr) | ((score == thr) & (key_pos <= cut))


def _prompt_attn_kernel(ii_ref, jj_ref, nj_ref, kb_ref, qt_ref, vt_ref, st_ref, sel_ref, bias_ref,
                        o_ref, m_s, l_s, acc_s, *, tk, tq, n_kind, d0_min):
    s = pl.program_id(1)
    i = ii_ref[s]
    j = jj_ref[s]
    nj = nj_ref[s]

    @pl.when(j == 0)
    def _():
        m_s[...] = jnp.full(m_s.shape, NEG, F32)
        l_s[...] = jnp.zeros(l_s.shape, F32)
        acc_s[...] = jnp.zeros(acc_s.shape, F32)

    key_pos = (j * tk + lax.broadcasted_iota(I32, (tk, tq), 0)).astype(F32)
    keep = _selected(st_ref[0], key_pos, sel_ref[0, 0:1, :], sel_ref[0, 1:2, :])
    mask_bias = jnp.where(keep, 0.0, NEG)
    kind = jnp.clip((i * tq - j * tk - d0_min) // tk, 0, n_kind - 1)
    for h in range(N_HEADS_A):
        hs = slice(h * HEAD_DIM, (h + 1) * HEAD_DIM)
        lg = jnp.dot(kb_ref[0, :, hs], qt_ref[0, hs, :], preferred_element_type=F32)
        lg = lg + bias_ref[kind, h] + mask_bias
        m_old = m_s[h:h + 1, :]
        m_new = jnp.maximum(m_old, jnp.max(lg, axis=0, keepdims=True))
        alpha = jnp.exp(m_old - m_new)
        p = jnp.exp(lg - m_new)
        l_s[h:h + 1, :] = alpha * l_s[h:h + 1, :] + jnp.sum(p, axis=0, keepdims=True)
        pv = jnp.dot(vt_ref[0, hs, :], p.astype(BF16), preferred_element_type=F32)
        acc_s[hs, :] = alpha * acc_s[hs, :] + pv
        m_s[h:h + 1, :] = m_new

    @pl.when(j == nj - 1)
    def _():
        for h in range(N_HEADS_A):
            hs = slice(h * HEAD_DIM, (h + 1) * HEAD_DIM)
            o_ref[0, hs, :] = (acc_s[hs, :] / l_s[h:h + 1, :]).astype(o_ref.dtype)


def _prompt_attention(kb, qt, vt, st, sel, bias_tiles, n_kind, d0_min, tk, tq):
    b, s, d = kb.shape
    ii, jj = _tri_tables(s // tq, tq, tk)
    nj = np.asarray([(i + 1) * tq // tk for i in ii], np.int32)
    grid_spec = pltpu.PrefetchScalarGridSpec(
        num_scalar_prefetch=3,
        grid=(b, len(ii)),
        in_specs=[pl.BlockSpec((1, tk, d), lambda bb, t, ii, jj, nj: (bb, jj[t], 0)),
                  pl.BlockSpec((1, d, tq), lambda bb, t, ii, jj, nj: (bb, 0, ii[t])),
                  pl.BlockSpec((1, d, tk), lambda bb, t, ii, jj, nj: (bb, 0, jj[t])),
                  pl.BlockSpec((1, tk, tq), lambda bb, t, ii, jj, nj: (bb, jj[t], ii[t])),
                  pl.BlockSpec((1, 8, tq), lambda bb, t, ii, jj, nj: (bb, 0, ii[t])),
                  pl.BlockSpec((n_kind, N_HEADS_A, tk, tq), lambda bb, t, ii, jj, nj: (0, 0, 0, 0))],
        out_specs=pl.BlockSpec((1, d, tq), lambda bb, t, ii, jj, nj: (bb, 0, ii[t])),
        scratch_shapes=[pltpu.VMEM((N_HEADS_A, tq), F32), pltpu.VMEM((N_HEADS_A, tq), F32),
                        pltpu.VMEM((d, tq), F32)],
    )
    return pl.pallas_call(
        functools.partial(_prompt_attn_kernel, tk=tk, tq=tq, n_kind=n_kind, d0_min=d0_min),
        grid_spec=grid_spec,
        out_shape=jax.ShapeDtypeStruct((b, d, s), BF16),
        compiler_params=_params(("parallel", "arbitrary")),
        name="prompt_sparse_attention",
    )(jnp.asarray(ii), jnp.asarray(jj), jnp.asarray(nj), kb, qt, vt, st, sel, bias_tiles)


PAGES_PER_STEP = 8


def _page_spec(block, u):
    zeros = (0,) * (len(block) - 1)
    return pl.BlockSpec(block, lambda sq, g, pt: (pt[sq, g * PAGES_PER_STEP + u],) + zeros)


def _head_sum(x, t_new):
    acc = x[0:t_new, :]
    for h in range(1, N_IDX_HEADS):
        acc = acc + x[h * t_new:(h + 1) * t_new, :]
    return acc


def _sample_scores_kernel(pt_ref, *refs, t_new):
    pages = refs[:PAGES_PER_STEP]
    qi_ref, w_ref, kinew_ref, o_ref, onew_ref = refs[PAGES_PER_STEP:]
    qi = qi_ref[0]
    w = w_ref[0]

    def score(keys_bf):
        d = lax.dot_general(qi, keys_bf, (((1,), (1,)), ((), ())), preferred_element_type=F32)
        return _head_sum(jnp.maximum(d, 0.0) * w, t_new) * INDEX_SCALE

    for u in range(PAGES_PER_STEP):
        o_ref[0, :, u * PAGE_SIZE:(u + 1) * PAGE_SIZE] = score(pages[u][0].astype(BF16))

    sc = score(kinew_ref[0])
    tq = lax.broadcasted_iota(I32, (t_new, PAGE_SIZE), 0)
    ss = lax.broadcasted_iota(I32, (t_new, PAGE_SIZE), 1)
    onew_ref[0] = jnp.where(ss <= tq, sc, -jnp.inf)


def _sample_scores(page_table, cache_idx, qi_rows, w_rows, ki_new_pad, t_new):
    bd, n_pages = page_table.shape
    rows = N_IDX_HEADS * t_new
    per_seq = lambda shp: pl.BlockSpec((1,) + shp, lambda sq, g, pt: (sq, 0, 0))
    grid_spec = pltpu.PrefetchScalarGridSpec(
        num_scalar_prefetch=1,
        grid=(bd, n_pages // PAGES_PER_STEP),
        in_specs=[_page_spec((1, PAGE_SIZE, D_IDX), u) for u in range(PAGES_PER_STEP)]
        + [per_seq((rows, D_IDX)), per_seq((rows, 1)), per_seq((PAGE_SIZE, D_IDX))],
        out_specs=[pl.BlockSpec((1, t_new, PAGES_PER_STEP * PAGE_SIZE), lambda sq, g, pt: (sq, 0, g)),
                   per_seq((t_new, PAGE_SIZE))],
    )
    return pl.pallas_call(
        functools.partial(_sample_scores_kernel, t_new=t_new),
        grid_spec=grid_spec,
        out_shape=[jax.ShapeDtypeStruct((bd, t_new, n_pages * PAGE_SIZE), F32),
                   jax.ShapeDtypeStruct((bd, t_new, PAGE_SIZE), F32)],
        compiler_params=_params(("parallel", "arbitrary")),
        name="sample_indexer_scores",
    )(page_table, *([cache_idx] * PAGES_PER_STEP), qi_rows, w_rows, ki_new_pad)


def _sample_attn_kernel(pt_ref, *refs, t_new, n_groups, far_pages):
    kp = refs[:PAGES_PER_STEP]
    vp = refs[PAGES_PER_STEP:2 * PAGES_PER_STEP]
    (q_ref, knew_ref, vnew_ref, sc_ref, scnew_ref, sel_ref, bias_ref, o_ref,
     qb_s, m_s, l_s, acc_s) = refs[2 * PAGES_PER_STEP:]
    g = pl.program_id(1)
    rows = N_HEADS_A * t_new
    d = N_HEADS_A * HEAD_DIM

    @pl.when(g == 0)
    def _():
        q_all = jnp.concatenate([q_ref[0].astype(F32)] * N_HEADS_A, axis=0)
        r = lax.broadcasted_iota(I32, (rows, d), 0)
        c = lax.broadcasted_iota(I32, (rows, d), 1)
        qb_s[...] = jnp.where(r // t_new == c // HEAD_DIM, q_all, 0.0).astype(BF16)
        m_s[...] = jnp.full(m_s.shape, NEG, F32)
        l_s[...] = jnp.zeros(l_s.shape, F32)
        acc_s[...] = jnp.zeros(acc_s.shape, F32)

    thr = sel_ref[0, :, 0:1]
    cut = sel_ref[0, :, 1:2]
    qb = qb_s[...]

    def logits(keys_bf, score, key_pos0, bias):
        lg = lax.dot_general(qb, keys_bf, (((1,), (1,)), ((), ())), preferred_element_type=F32)
        pos = (key_pos0 + lax.broadcasted_iota(I32, (t_new, PAGE_SIZE), 1)).astype(F32)
        mb = jnp.where(_selected(score, pos, thr, cut), 0.0, NEG)
        return lg + bias + jnp.concatenate([mb] * N_HEADS_A, axis=0)

    def update(lg, values_bf):
        m_old = m_s[...]
        m_new = jnp.maximum(m_old, jnp.max(lg, axis=1, keepdims=True))
        alpha = jnp.exp(m_old - m_new)
        p = jnp.exp(lg - m_new)
        l_s[...] = alpha * l_s[...] + jnp.sum(p, axis=1, keepdims=True)
        pv = jnp.zeros((rows, d), F32)
        for u, vb in enumerate(values_bf):
            pv = pv + jnp.dot(p[:, u * PAGE_SIZE:(u + 1) * PAGE_SIZE].astype(BF16), vb,
                              preferred_element_type=F32)
        acc_s[...] = alpha * acc_s[...] + pv
        m_s[...] = m_new

    lgs = []
    for u in range(PAGES_PER_STEP):
        page = g * PAGES_PER_STEP + u
        bias = bias_ref[jnp.where(page < far_pages, 0, 1)]
        lgs.append(logits(kp[u][0].astype(BF16), sc_ref[0, :, u * PAGE_SIZE:(u + 1) * PAGE_SIZE],
                          page * PAGE_SIZE, bias))
    update(jnp.concatenate(lgs, axis=1), [vp[u][0].astype(BF16) for u in range(PAGES_PER_STEP)])

    @pl.when(g == n_groups - 1)
    def _():
        past = n_groups * PAGES_PER_STEP * PAGE_SIZE
        update(logits(knew_ref[0], scnew_ref[0], past, bias_ref[2]), [vnew_ref[0]])
        out = []
        for h in range(N_HEADS_A):
            rs = slice(h * t_new, (h + 1) * t_new)
            out.append(acc_s[rs, h * HEAD_DIM:(h + 1) * HEAD_DIM] / l_s[rs, :])
        o_ref[0] = jnp.concatenate(out, axis=1).astype(o_ref.dtype)


def _sample_attention(page_table, cache_k, cache_v, q_s, k_new_pad, v_new_pad, sc_past, sc_new,
                      sel_rows, bias_tiles, t_new):
    bd, n_pages = page_table.shape
    d = cache_k.shape[-1]
    rows = N_HEADS_A * t_new
    n_groups = n_pages // PAGES_PER_STEP
    far_pages = n_pages - 1
    assert PAGE_SIZE + 1 >= FAR_DISTANCE
    per_seq = lambda shp: pl.BlockSpec((1,) + shp, lambda sq, g, pt: (sq, 0, 0))
    grid_spec = pltpu.PrefetchScalarGridSpec(
        num_scalar_prefetch=1,
        grid=(bd, n_groups),
        in_specs=[_page_spec((1, PAGE_SIZE, d), u) for u in range(PAGES_PER_STEP)] * 2
        + [per_seq((t_new, d)), per_seq((PAGE_SIZE, d)), per_seq((PAGE_SIZE, d)),
           pl.BlockSpec((1, t_new, PAGES_PER_STEP * PAGE_SIZE), lambda sq, g, pt: (sq, 0, g)),
           per_seq((t_new, PAGE_SIZE)), per_seq((t_new, 128)),
           pl.BlockSpec((3, rows, PAGE_SIZE), lambda sq, g, pt: (0, 0, 0))],
        out_specs=per_seq((t_new, d)),
        scratch_shapes=[pltpu.VMEM((rows, d), BF16), pltpu.VMEM((rows, 1), F32),
                        pltpu.VMEM((rows, 1), F32), pltpu.VMEM((rows, d), F32)],
    )
    return pl.pallas_call(
        functools.partial(_sample_attn_kernel, t_new=t_new, n_groups=n_groups, far_pages=far_pages),
        grid_spec=grid_spec,
        out_shape=jax.ShapeDtypeStruct((bd, t_new, d), BF16),
        compiler_params=_params(("parallel", "arbitrary")),
        name="sample_sparse_attention",
    )(page_table, *([cache_k] * PAGES_PER_STEP), *([cache_v] * PAGES_PER_STEP), q_s, k_new_pad,
      v_new_pad, sc_past, sc_new, sel_rows, bias_tiles)


def _ffn_kernel(*refs, tm, d_ff, tiles_per_seq, fixup):
    if fixup:
        (x_ref, a_ref, r_ref, wo_ref, gf_ref, wup_ref, cw_ref, cb_ref, wdn_ref, gl_ref, x0_ref, x1_ref,
         y_ref, g_ref, gbuf) = refs
    else:
        (x_ref, a_ref, r_ref, wo_ref, gf_ref, wup_ref, cw_ref, cb_ref, wdn_ref, gl_ref,
         y_ref, g_ref, gbuf) = refs
    i = pl.program_id(0)
    d_attn = a_ref.shape[1]
    x1 = (x_ref[...] + jnp.dot(a_ref[...], wo_ref[0:d_attn, :], preferred_element_type=F32)
          + jnp.dot(r_ref[...], wo_ref[d_attn:, :], preferred_element_type=F32))
    xn = _rmsnorm(x1, gf_ref[...]).astype(BF16)
    up = jnp.dot(xn, wup_ref[...], preferred_element_type=F32)
    g = up[:, :d_ff]
    u = up[:, d_ff:]

    @pl.when(i % tiles_per_seq == 0)
    def _():
        gbuf[0:8, :] = jnp.zeros((8, d_ff), F32)

    gbuf[8:8 + tm, :] = g
    prev2 = gbuf[6:6 + tm, :]
    prev1 = gbuf[7:7 + tm, :]
    if fixup:
        t = lax.broadcasted_iota(I32, (tm, d_ff), 0) % 8
        prev2 = jnp.where(t < 2, x0_ref[...], prev2)
        prev1 = jnp.where(t < 1, x1_ref[...], prev1)
    gc = prev2 * cw_ref[0:1, :] + cb_ref[...]
    gc = gc + prev1 * cw_ref[1:2, :]
    gc = gc + g * cw_ref[2:3, :]
    hid = (jax.nn.gelu(gc) * u).astype(BF16)
    x2 = x1 + jnp.dot(hid, wdn_ref[...], preferred_element_type=F32)
    y_ref[...] = _rmsnorm(x2, gl_ref[...])
    if fixup:
        g_ref[...] = g
    else:
        g_ref[0] = g[tm - (FFN_CONV_W - 1):, :]
    gbuf[0:8, :] = gbuf[tm:tm + 8, :]


def _merge_and_ffn(x2d, attn, lru, w_out, norm_ffn, w_up, conv_w, conv_b, w_down, norm_final, tm,
                   rows_per_seq, fix0=None, fix1=None):
    n, d = x2d.shape
    d_ff = w_down.shape[0]
    fixup = fix0 is not None
    tiles_per_seq = max(rows_per_seq // tm, 1)
    row = lambda w: pl.BlockSpec((tm, w), lambda i: (i, 0))
    const = lambda a: pl.BlockSpec(a.shape, lambda i: (0,) * a.ndim, pipeline_mode=pl.Buffered(1))
    cb2 = conv_b.reshape(1, d_ff)
    gf2 = norm_ffn.reshape(1, d)
    gl2 = norm_final.reshape(1, d)
    ins = [x2d, attn, lru, w_out, gf2, w_up, conv_w, cb2, w_down, gl2]
    in_specs = [row(d), row(attn.shape[1]), row(lru.shape[1]), const(w_out), const(gf2), const(w_up),
                const(conv_w), const(cb2), const(w_down), const(gl2)]
    if fixup:
        ins += [fix0, fix1]
        in_specs += [row(d_ff), row(d_ff)]
        g_spec = row(d_ff)
        g_shape = jax.ShapeDtypeStruct((n, d_ff), F32)
    else:
        g_spec = pl.BlockSpec((1, FFN_CONV_W - 1, d_ff), lambda i: (i // tiles_per_seq, 0, 0))
        g_shape = jax.ShapeDtypeStruct((n // rows_per_seq, FFN_CONV_W - 1, d_ff), F32)
    return pl.pallas_call(
        functools.partial(_ffn_kernel, tm=tm, d_ff=d_ff, tiles_per_seq=tiles_per_seq, fixup=fixup),
        grid=(n // tm,),
        in_specs=in_specs,
        out_specs=[row(d), g_spec],
        out_shape=[jax.ShapeDtypeStruct((n, d), F32), g_shape],
        scratch_shapes=[pltpu.VMEM((tm + 8, d_ff), F32)],
        compiler_params=_params(("arbitrary",)),
        name="merge_and_conv_ffn",
    )(*ins)


def _block_diag(w):
    h, a, b = w.shape
    eye = jnp.eye(h, dtype=w.dtype)
    return (w[:, :, None, :] * eye[:, None, :, None]).reshape(h * a, h * b)


def kernel(x_prompt, x_sample, cache_k, cache_v, cache_idx_k, state_lru_h, state_lru_conv, state_ffn_conv, page_table, w_in, w_out, norm_mix, norm_ffn, norm_final, rel_bias, lru_conv_w, lru_conv_b, lru_w_a, lru_b_a, lru_w_x, lru_b_x, lru_lambda, ffn_w_up, ffn_conv_w, ffn_conv_b, ffn_w_down):
    bp, s, d = x_prompt.shape
    bd, t_new, _ = x_sample.shape
    depth = w_in.shape[0]
    assert depth == 1 and t_new == 8
    d_lru = lru_lambda.shape[1]
    d_ff = ffn_w_down.shape[1]
    n_pages = page_table.shape[1]
    past = n_pages * PAGE_SIZE
    l = 0

    c_qi_end = 3 * D_ATTN + D_QI
    c_wi_end = c_qi_end + D_IDX + N_IDX_HEADS
    pad_cols = (-(D_IDX + N_IDX_HEADS)) % 128
    w_re = jnp.concatenate([w_in[l][:, :c_qi_end], w_in[l][:, c_wi_end:], w_in[l][:, c_qi_end:c_wi_end],
                            jnp.zeros((d, pad_cols), w_in.dtype)], axis=1).astype(BF16)
    wax = jnp.concatenate([_block_diag(lru_w_a[l]), _block_diag(lru_w_x[l])], axis=1).astype(BF16)
    bax = jnp.concatenate([lru_b_a[l], lru_b_x[l]])
    w_out_b = w_out[l].astype(BF16)
    w_up_b = ffn_w_up[l].astype(BF16)
    w_dn_b = ffn_w_down[l].astype(BF16)

    tk = tq = 256
    (qb, k, v, kb, vb, qib, ki, kib, wi, lx, lg) = _in_projection(
        x_prompt.reshape(bp * s, d), norm_mix[l], w_re, d_lru, tm=512)
    r3 = lambda a: a.reshape(bp, s, a.shape[-1])
    tr = lambda a: jnp.swapaxes(r3(a), 1, 2)
    st = _prompt_scores(r3(kib), tr(qib), tr(wi), tk, tq)
    sel = _topk_threshold(st, tk, tq, min(TOPK_MAX, s // 4), causal=True)
    bias_tiles, n_kind, d0_min = _prompt_bias_tiles(rel_bias, tk, tq)
    attn_t = _prompt_attention(r3(kb), tr(qb), tr(vb), st, sel, bias_tiles, n_kind, d0_min, tk, tq)
    attn_p = jnp.swapaxes(attn_t, 1, 2).reshape(bp * s, D_ATTN)
    lru_p, lconv_p, hlast_p = _lru_branch(
        r3(lx), r3(lg), jnp.zeros((bp, 8, d_lru), F32), jnp.zeros((bp, 1, d_lru), F32),
        lru_conv_w[l], lru_conv_b[l], wax, bax, lru_lambda[l], tc=512)
    y_p, fconv_p = _merge_and_ffn(
        x_prompt.reshape(bp * s, d), attn_p, lru_p.reshape(bp * s, d_lru), w_out_b, norm_ffn[l], w_up_b,
        ffn_conv_w[l], ffn_conv_b[l], w_dn_b, norm_final, tm=256, rows_per_seq=s)
    k_p, v_p, ki_p = k, v, ki

    n_s = bd * t_new
    (qb, k, v, kb, vb, qib, ki, kib, wi, lx, lg) = _in_projection(
        x_sample.reshape(n_s, d), norm_mix[l], w_re, d_lru, tm=min(512, n_s))
    s3 = lambda a: a.reshape(bd, t_new, a.shape[-1])
    qi_rows = s3(qib).reshape(bd, t_new, N_IDX_HEADS, D_IDX).transpose(0, 2, 1, 3).reshape(
        bd, N_IDX_HEADS * t_new, D_IDX)
    w_rows = s3(wi).transpose(0, 2, 1).reshape(bd, N_IDX_HEADS * t_new, 1)
    pad_new = lambda a: jnp.pad(s3(a), ((0, 0), (0, PAGE_SIZE - t_new), (0, 0)))
    sc_past, sc_new = _sample_scores(page_table, cache_idx_k[l], qi_rows, w_rows, pad_new(kib), t_new)
    sc_all = jnp.concatenate([sc_past, sc_new], axis=2).reshape(n_s, past + PAGE_SIZE)
    sel_s = _topk_threshold(sc_all.T[None], PAGE_SIZE, min(256, n_s), min(TOPK_MAX, (past + t_new) // 4),
                            causal=False)
    sel_rows = jnp.pad(sel_s[0].T.reshape(bd, t_new, 8), ((0, 0), (0, 0), (0, 120)))
    attn_s = _sample_attention(
        page_table, cache_k[l].reshape(-1, PAGE_SIZE, D_ATTN), cache_v[l].reshape(-1, PAGE_SIZE, D_ATTN),
        s3(qb), pad_new(kb), pad_new(vb), sc_past, sc_new, sel_rows,
        _sample_bias_tiles(rel_bias, t_new), t_new)
    buf8 = jnp.pad(state_lru_conv[l], ((0, 0), (8 - (LRU_CONV_W - 1), 0), (0, 0)))
    lru_s, lconv_s, hlast_s = _lru_branch(
        s3(lx), s3(lg), buf8, state_lru_h[l][:, None, :], lru_conv_w[l], lru_conv_b[l], wax, bax,
        lru_lambda[l], tc=t_new)
    fbuf = state_ffn_conv[l]
    fix0 = jnp.pad(fbuf, ((0, 0), (0, t_new - 2), (0, 0))).reshape(n_s, d_ff)
    fix1 = jnp.pad(fbuf[:, 1:2], ((0, 0), (0, t_new - 1), (0, 0))).reshape(n_s, d_ff)
    y_s, g_s = _merge_and_ffn(
        x_sample.reshape(n_s, d), attn_s.reshape(n_s, D_ATTN), lru_s.reshape(n_s, d_lru), w_out_b,
        norm_ffn[l], w_up_b, ffn_conv_w[l], ffn_conv_b[l], w_dn_b, norm_final, tm=min(256, n_s),
        rows_per_seq=n_s, fix0=fix0, fix1=fix1)
    fconv_s = g_s.reshape(bd, t_new, d_ff)[:, t_new - (FFN_CONV_W - 1):, :]

    heads = lambda a, b_, t_: a.reshape(1, b_, t_, N_HEADS_A, HEAD_DIM)
    return (y_p.reshape(bp, s, d), y_s.reshape(bd, t_new, d),
            heads(k_p, bp, s), heads(v_p, bp, s), ki_p.reshape(1, bp, s, D_IDX),
            hlast_p.reshape(1, bp, d_lru), lconv_p[None], fconv_p[None],
            heads(k, bd, t_new), heads(v, bd, t_new), ki.reshape(1, bd, t_new, D_IDX),
            hlast_s.reshape(1, bd, d_lru), lconv_s[None], fconv_s[None])
```

```python
import functools
import math

import jax
import jax.numpy as jnp
import numpy as np
from jax import lax
from jax.experimental import pallas as pl
from jax.experimental.pallas import tpu as pltpu

F32 = jnp.float32
BF16 = jnp.bfloat16
I32 = jnp.int32

HEAD_DIM = 64
N_HEADS_A = 8
D_ATTN = HEAD_DIM * N_HEADS_A
N_IDX_HEADS = 8
D_IDX = 64
D_QI = N_IDX_HEADS * D_IDX
INDEX_SCALE = (N_IDX_HEADS * D_IDX) ** -0.5
TOPK_MAX = 256
N_HEADS_LRU = 8
LRU_CONV_W = 4
LRU_C = 8.0
N_BUCKETS = 32
MAX_DISTANCE = 128
FFN_CONV_W = 3
EPS = 1e-6
PAGE_SIZE = 128

FAR_DISTANCE = 113
NEG = -1e30
NO_TIE_CUT = float(2 ** 30)

VMEM_LIMIT_BYTES = 56 * 1024 * 1024


def _params(sem):
    return pltpu.CompilerParams(dimension_semantics=sem, vmem_limit_bytes=VMEM_LIMIT_BYTES)


def _rmsnorm(x, g):
    ms = jnp.mean(x * x, axis=-1, keepdims=True)
    return (x * lax.rsqrt(ms + EPS)) * g


def _inproj_kernel(x_ref, g_ref, w_ref, qb_ref, k_ref, v_ref, kb_ref, vb_ref, qib_ref, ki_ref,
                   kib_ref, wi_ref, lx_ref, lg_ref, *, d_lru):
    xn = _rmsnorm(x_ref[...], g_ref[...])
    z = jnp.dot(xn.astype(BF16), w_ref[...], preferred_element_type=F32)
    o = 0
    q = z[:, o:o + D_ATTN]; o += D_ATTN
    k = z[:, o:o + D_ATTN]; o += D_ATTN
    v = z[:, o:o + D_ATTN]; o += D_ATTN
    qi = z[:, o:o + D_QI]; o += D_QI
    lx = z[:, o:o + d_lru]; o += d_lru
    lg = z[:, o:o + d_lru]; o += d_lru
    ki = z[:, o:o + D_IDX]; o += D_IDX
    wi = z[:, o:o + N_IDX_HEADS]
    qb_ref[...] = (q * (HEAD_DIM ** -0.5)).astype(BF16)
    k_ref[...] = k
    v_ref[...] = v
    kb_ref[...] = k.astype(BF16)
    vb_ref[...] = v.astype(BF16)
    qib_ref[...] = qi.astype(BF16)
    ki_ref[...] = ki
    kib_ref[...] = ki.astype(BF16)
    wi_ref[...] = wi
    lx_ref[...] = lx
    lg_ref[...] = lg


def _in_projection(x2d, norm_g, w_re, d_lru, tm):
    n, d = x2d.shape
    nc = w_re.shape[1]
    row = lambda w: pl.BlockSpec((tm, w), lambda i: (i, 0))
    widths = [(D_ATTN, BF16), (D_ATTN, F32), (D_ATTN, F32), (D_ATTN, BF16), (D_ATTN, BF16),
              (D_QI, BF16), (D_IDX, F32), (D_IDX, BF16), (N_IDX_HEADS, F32), (d_lru, F32), (d_lru, F32)]
    return pl.pallas_call(
        functools.partial(_inproj_kernel, d_lru=d_lru),
        grid=(n // tm,),
        in_specs=[row(d), pl.BlockSpec((1, d), lambda i: (0, 0)),
                  pl.BlockSpec((d, nc), lambda i: (0, 0))],
        out_specs=[row(w) for w, _ in widths],
        out_shape=[jax.ShapeDtypeStruct((n, w), dt) for w, dt in widths],
        compiler_params=_params(("parallel",)),
        name="in_projection",
    )(x2d, norm_g.reshape(1, d), w_re)


def _softplus(x):
    return jnp.maximum(x, 0.0) + jnp.log1p(jnp.exp(-jnp.abs(x)))


def _lru_kernel(lx_ref, lg_ref, buf_ref, h0_ref, cw_ref, cb_ref, wax_ref, bax_ref, lam_ref,
                y_ref, newbuf_ref, hlast_ref, xbuf, hcar, a_s, h_s, *, tc, d_lru):
    c = pl.program_id(1)

    @pl.when(c == 0)
    def _():
        xbuf[0:8, :] = buf_ref[0]
        hcar[...] = h0_ref[0]

    xbuf[8:8 + tc, :] = lx_ref[0]
    xc = xbuf[5:5 + tc, :] * cw_ref[0:1, :] + cb_ref[...]
    for j in range(1, LRU_CONV_W):
        xc = xc + xbuf[5 + j:5 + j + tc, :] * cw_ref[j:j + 1, :]
    gates = jnp.dot(xc.astype(BF16), wax_ref[...], preferred_element_type=F32) + bax_ref[...]
    r = jax.nn.sigmoid(gates[:, :d_lru])
    ig = jax.nn.sigmoid(gates[:, d_lru:])
    log_a = -LRU_C * r * _softplus(-lam_ref[...])
    a_s[...] = jnp.exp(log_a)
    neg_expm1 = -jnp.tanh(log_a) * (jnp.exp(2.0 * log_a) + 1.0)
    h_s[...] = jnp.sqrt(neg_expm1) * ig * xc

    def group(gi, h):
        r0 = pl.multiple_of(gi * 8, 8)
        a8 = a_s[pl.ds(r0, 8), :]
        b8 = h_s[pl.ds(r0, 8), :]
        rows = []
        for t in range(8):
            h = a8[t:t + 1, :] * h + b8[t:t + 1, :]
            rows.append(h)
        h_s[pl.ds(r0, 8), :] = jnp.concatenate(rows, axis=0)
        return h

    h = lax.fori_loop(0, tc // 8, group, hcar[...])
    hcar[...] = h
    y_ref[0] = (h_s[...] * jax.nn.gelu(lg_ref[0])).astype(y_ref.dtype)
    newbuf_ref[0] = xbuf[tc + 5:tc + 8, :]
    hlast_ref[0] = h
    xbuf[0:8, :] = xbuf[tc:tc + 8, :]


def _lru_branch(lx, lg, buf8, h0, conv_w, conv_b, wax, bax, lam, tc):
    b, t, c = lx.shape
    seq = lambda: pl.BlockSpec((1, tc, c), lambda i, j: (i, j, 0))
    per_b = lambda r: pl.BlockSpec((1, r, c), lambda i, j: (i, 0, 0))
    const = lambda shp: pl.BlockSpec(shp, lambda i, j: (0, 0))
    return pl.pallas_call(
        functools.partial(_lru_kernel, tc=tc, d_lru=c),
        grid=(b, t // tc),
        in_specs=[seq(), seq(), per_b(8), per_b(1), const((LRU_CONV_W, c)), const((1, c)),
                  const((c, 2 * c)), const((1, 2 * c)), const((1, c))],
        out_specs=[seq(), per_b(LRU_CONV_W - 1), per_b(1)],
        out_shape=[jax.ShapeDtypeStruct((b, t, c), BF16),
                   jax.ShapeDtypeStruct((b, LRU_CONV_W - 1, c), F32),
                   jax.ShapeDtypeStruct((b, 1, c), F32)],
        scratch_shapes=[pltpu.VMEM((tc + 8, c), F32), pltpu.VMEM((1, c), F32),
                        pltpu.VMEM((tc, c), F32), pltpu.VMEM((tc, c), F32)],
        compiler_params=_params(("parallel", "arbitrary")),
        name="rglru_branch",
    )(lx, lg, buf8, h0, conv_w, conv_b.reshape(1, c), wax, bax.reshape(1, 2 * c), lam.reshape(1, c))


def _t5_bucket(dist):
    n = jnp.maximum(dist, 0)
    max_exact = N_BUCKETS // 2
    nf = jnp.maximum(n, 1).astype(F32)
    large = max_exact + (jnp.log(nf / max_exact) / math.log(MAX_DISTANCE / max_exact)
                         * (N_BUCKETS - max_exact)).astype(I32)
    large = jnp.minimum(large, N_BUCKETS - 1)
    return jnp.where(n < max_exact, n, large)


def _bias_lookup(rb_ref, bucket, head):
    acc = jnp.zeros(bucket.shape, F32)
    for b in range(N_BUCKETS):
        acc = jnp.where(bucket == b, rb_ref[b * N_HEADS_A + head], acc)
    return acc


def _bias_tiles_kernel(rb_ref, o_ref, *, tk, tq, d0_min):
    kind = pl.program_id(0)
    head = pl.program_id(1)
    ss = lax.broadcasted_iota(I32, (tk, tq), 0)
    tt = lax.broadcasted_iota(I32, (tk, tq), 1)
    d = d0_min + kind * tk + tt - ss
    val = _bias_lookup(rb_ref, _t5_bucket(d), head)
    o_ref[0, 0] = jnp.where(d < 0, NEG, val)


def _prompt_bias_tiles(rel_bias, tk, tq):
    d0_min = tk - tq
    n_near = -(-(tk - 1 + FAR_DISTANCE - d0_min) // tk)
    tiles = pl.pallas_call(
        functools.partial(_bias_tiles_kernel, tk=tk, tq=tq, d0_min=d0_min),
        grid=(n_near, N_HEADS_A),
        in_specs=[pl.BlockSpec(memory_space=pltpu.SMEM)],
        out_specs=pl.BlockSpec((1, 1, tk, tq), lambda k, h: (k, h, 0, 0)),
        out_shape=jax.ShapeDtypeStruct((n_near, N_HEADS_A, tk, tq), F32),
        compiler_params=_params(("parallel", "parallel")),
        name="prompt_bias_tiles",
    )(rel_bias.reshape(-1))
    return tiles, n_near, d0_min


def _sample_bias_kernel(rb_ref, o_ref, *, t_new):
    kind = pl.program_id(0)
    rows = N_HEADS_A * t_new
    r = lax.broadcasted_iota(I32, (rows, PAGE_SIZE), 0)
    ss = lax.broadcasted_iota(I32, (rows, PAGE_SIZE), 1)
    tq = r % t_new
    head = r // t_new
    base = jnp.where(kind == 0, FAR_DISTANCE + PAGE_SIZE, jnp.where(kind == 1, PAGE_SIZE, 0))
    d = base + tq - ss
    bucket = _t5_bucket(d)
    acc = jnp.zeros((rows, PAGE_SIZE), F32)
    for h in range(N_HEADS_A):
        acc = jnp.where(head == h, _bias_lookup(rb_ref, bucket, h), acc)
    o_ref[0] = jnp.where(d < 0, NEG, acc)


def _sample_bias_tiles(rel_bias, t_new):
    rows = N_HEADS_A * t_new
    return pl.pallas_call(
        functools.partial(_sample_bias_kernel, t_new=t_new),
        grid=(3,),
        in_specs=[pl.BlockSpec(memory_space=pltpu.SMEM)],
        out_specs=pl.BlockSpec((1, rows, PAGE_SIZE), lambda k: (k, 0, 0)),
        out_shape=jax.ShapeDtypeStruct((3, rows, PAGE_SIZE), F32),
        compiler_params=_params(("parallel",)),
        name="sample_bias_tiles",
    )(rel_bias.reshape(-1))


def _tri_tables(n_q_blocks, tq, tk):
    ii, jj = [], []
    for i in range(n_q_blocks):
        for j in range((i + 1) * tq // tk):
            ii.append(i)
            jj.append(j)
    return np.asarray(ii, np.int32), np.asarray(jj, np.int32)


def _scores_kernel(ii_ref, jj_ref, ki_ref, qit_ref, wit_ref, o_ref, *, tk, tq):
    s = pl.program_id(1)
    i = ii_ref[s]
    j = jj_ref[s]
    ki = ki_ref[0]
    acc = jnp.zeros((tk, tq), F32)
    for h in range(N_IDX_HEADS):
        d = jnp.dot(ki, qit_ref[0, h * D_IDX:(h + 1) * D_IDX, :], preferred_element_type=F32)
        acc = acc + jnp.maximum(d, 0.0) * wit_ref[0, h:h + 1, :]
    sc = acc * INDEX_SCALE
    kpos = j * tk + lax.broadcasted_iota(I32, (tk, tq), 0)
    qpos = i * tq + lax.broadcasted_iota(I32, (tk, tq), 1)
    o_ref[0] = jnp.where(kpos <= qpos, sc, -jnp.inf)


def _prompt_scores(kib, qit, wit, tk, tq):
    b, s, _ = kib.shape
    ii, jj = _tri_tables(s // tq, tq, tk)
    grid_spec = pltpu.PrefetchScalarGridSpec(
        num_scalar_prefetch=2,
        grid=(b, len(ii)),
        in_specs=[pl.BlockSpec((1, tk, D_IDX), lambda bb, st, ii, jj: (bb, jj[st], 0)),
                  pl.BlockSpec((1, D_QI, tq), lambda bb, st, ii, jj: (bb, 0, ii[st])),
                  pl.BlockSpec((1, N_IDX_HEADS, tq), lambda bb, st, ii, jj: (bb, 0, ii[st]))],
        out_specs=pl.BlockSpec((1, tk, tq), lambda bb, st, ii, jj: (bb, jj[st], ii[st])),
    )
    return pl.pallas_call(
        functools.partial(_scores_kernel, tk=tk, tq=tq),
        grid_spec=grid_spec,
        out_shape=jax.ShapeDtypeStruct((b, s, s), F32),
        compiler_params=_params(("parallel", "arbitrary")),
        name="prompt_indexer_scores",
    )(jnp.asarray(ii), jnp.asarray(jj), kib, qit, wit)


def _to_key(x):
    bits = lax.bitcast_convert_type(x, I32)
    key = bits ^ ((bits >> 31) & 0x7FFFFFFF)
    return jnp.where(x == 0.0, 0, key)


def _from_key(key):
    bits = key ^ ((key >> 31) & 0x7FFFFFFF)
    return lax.bitcast_convert_type(bits, F32)


INT_MIN = -2 ** 31
INT_MAX = 2 ** 31 - 1


def _bisect_kth(count_ge, lo, hi, cnt_lo, cnt_hi, k_sel):
    def n_active(lo, hi, cnt_lo, cnt_hi):
        return jnp.max((((cnt_lo - cnt_hi) > 2) & (lo + 1 != hi)).astype(I32))

    def body(c):
        lo, hi, cnt_lo, cnt_hi, _ = c
        act = ((cnt_lo - cnt_hi) > 2) & (lo + 1 != hi)
        mid = _to_key(0.5 * _from_key(lo) + 0.5 * _from_key(hi))
        mid = jnp.where((lo < 0) & (hi > 0), 0, mid)
        mid = jnp.where((lo == 0) & (hi > 1), 1, mid)
        inside = (mid > lo) & (mid < hi)
        mid = jnp.where(inside, mid, (lo >> 1) + (hi >> 1) + (lo & hi & 1))
        cnt = count_ge(mid)
        up = act & (cnt >= k_sel)
        dn = act & (cnt < k_sel)
        lo = jnp.where(up, mid, lo)
        cnt_lo = jnp.where(up, cnt, cnt_lo)
        hi = jnp.where(dn, mid, hi)
        cnt_hi = jnp.where(dn, cnt, cnt_hi)
        return lo, hi, cnt_lo, cnt_hi, n_active(lo, hi, cnt_lo, cnt_hi)

    out = lax.while_loop(lambda c: c[4] > 0, body,
                         (lo, hi, cnt_lo, cnt_hi, n_active(lo, hi, cnt_lo, cnt_hi)))
    return out[:4]


def _threshold_kernel(ii_ref, jj_ref, nj_ref, st_ref, o_ref, slab, gmax, *, tk, tq, k_sel, rows, pos_bits):
    s = pl.program_id(1)
    j = jj_ref[s]
    nj = nj_ref[s]
    key = _to_key(st_ref[0])
    slab[j] = key

    @pl.when(j == 0)
    def _():
        gmax[...] = key

    @pl.when(j > 0)
    def _():
        gmax[...] = jnp.maximum(gmax[...], key)

    def reduce_rows(read, n_blocks, init, step, final):
        def body(jb, acc):
            for rc in range(tk // rows):
                acc = step(acc, read(jb, rc), jb, rc)
            return acc
        return final(lax.fori_loop(0, n_blocks, body, jnp.full((rows, tq), init, I32)))

    sum_rows = lambda a: jnp.sum(a, axis=0, keepdims=True)
    read_slab = lambda jb, rc: slab[jb, rc * rows:(rc + 1) * rows, :]
    read_gmax = lambda jb, rc: gmax[rc * rows:(rc + 1) * rows, :]

    def count(read, n_blocks, pred):
        return reduce_rows(read, n_blocks, 0, lambda acc, blk, jb, rc: acc + jnp.where(pred(blk, jb, rc), 1, 0),
                           sum_rows)

    @pl.when(j == nj - 1)
    def _():
        g = gmax[...]
        top = jnp.max(g, axis=0, keepdims=True) + 1
        zero = jnp.zeros((1, tq), I32)
        if tk >= k_sel:
            g_lo = jnp.min(g, axis=0, keepdims=True)
            lo, _, _, _ = _bisect_kth(lambda m: count(read_gmax, 1, lambda blk, jb, rc: blk >= m),
                                      g_lo, top, jnp.full((1, tq), tk, I32), zero, k_sel)
        else:
            lo = jnp.full((1, tq), INT_MIN, I32)
        count_ge = lambda m: count(read_slab, nj, lambda blk, jb, rc: blk >= m)
        lo, hi, cnt_lo, cnt_hi = _bisect_kth(count_ge, lo, top, count_ge(lo), zero, k_sel)
        mn = reduce_rows(read_slab, nj, INT_MAX,
                         lambda acc, blk, jb, rc: jnp.minimum(acc, jnp.where(blk >= lo, blk, INT_MAX)),
                         lambda a: jnp.min(a, axis=0, keepdims=True))
        mx = reduce_rows(read_slab, nj, INT_MIN,
                         lambda acc, blk, jb, rc: jnp.maximum(acc, jnp.where(blk < hi, blk, INT_MIN)),
                         lambda a: jnp.max(a, axis=0, keepdims=True))
        unit = lo + 1 == hi
        n_in = cnt_lo - cnt_hi
        rank = k_sel - cnt_hi
        kth = jnp.where(unit, lo, jnp.where(rank == 1, mx, mn))
        n_gt = jnp.where(unit | (rank == 1) | (mx == mn), cnt_hi, cnt_hi + 1)
        n_eq = jnp.where(unit, n_in, jnp.where((n_in == 2) & (mx == mn), 2, 1))
        need = k_sel - n_gt
        neg_inf_key = _to_key(jnp.full((1, tq), -jnp.inf, F32))
        tie_cut = (n_eq > need) & (kth != neg_inf_key)
        o_ref[0, 0:1, :] = _from_key(kth)
        o_ref[0, 1:2, :] = jnp.full((1, tq), NO_TIE_CUT, F32)
        o_ref[0, 2:8, :] = jnp.zeros((6, tq), F32)

        @pl.when(jnp.max(tie_cut.astype(I32)) > 0)
        def _():
            row = lax.broadcasted_iota(I32, (rows, tq), 0)

            def pos_step(it, cut):
                cand = cut | (jnp.int32(1) << (pos_bits - 1 - it))
                c = count(read_slab, nj,
                          lambda blk, jb, rc: (blk == kth) & (jb * tk + rc * rows + row < cand))
                return jnp.where(c < need, cand, cut)

            cut = lax.fori_loop(0, pos_bits, pos_step, jnp.zeros((1, tq), I32))
            o_ref[0, 1:2, :] = jnp.where(tie_cut, cut.astype(F32), NO_TIE_CUT)


def _topk_threshold(st, tk, tq, k_sel, causal):
    b, l, n = st.shape
    if causal:
        ii, jj = _tri_tables(n // tq, tq, tk)
    else:
        nb = l // tk
        ii = np.repeat(np.arange(n // tq, dtype=np.int32), nb)
        jj = np.tile(np.arange(nb, dtype=np.int32), n // tq)
    nj = np.asarray([(i + 1) * tq // tk if causal else l // tk for i in ii], np.int32)
    rows = 64
    assert tk % rows == 0
    grid_spec = pltpu.PrefetchScalarGridSpec(
        num_scalar_prefetch=3,
        grid=(b, len(ii)),
        in_specs=[pl.BlockSpec((1, tk, tq), lambda bb, st_, ii, jj, nj: (bb, jj[st_], ii[st_]))],
        out_specs=pl.BlockSpec((1, 8, tq), lambda bb, st_, ii, jj, nj: (bb, 0, ii[st_])),
        scratch_shapes=[pltpu.VMEM((int(nj.max()), tk, tq), I32), pltpu.VMEM((tk, tq), I32)],
    )
    return pl.pallas_call(
        functools.partial(_threshold_kernel, tk=tk, tq=tq, k_sel=k_sel, rows=rows,
                          pos_bits=max(1, (l - 1).bit_length())),
        grid_spec=grid_spec,
        out_shape=jax.ShapeDtypeStruct((b, 8, n), F32),
        compiler_params=_params(("parallel", "arbitrary")),
        name="topk_threshold",
    )(jnp.asarray(ii), jnp.asarray(jj), jnp.asarray(nj), st)


def _selected(score, key_pos, thr, cut):
    return (score > thr) | ((score == thr) & (key_pos <= cut))


ACC_ROWS = HEAD_DIM + 16


def _prompt_attn_kernel(ii_ref, jj_ref, nj_ref, cut_ref, kb_ref, qt_ref, vt_ref, st_ref, sel_ref, bias_ref,
                        far_ref, o_ref, m_s, acc_s, mb_s, lg_s, *, tk, tq, n_near, d0_min, n_q_blocks):
    b = pl.program_id(0)
    s = pl.program_id(1)
    i = ii_ref[s]
    j = jj_ref[s]
    nj = nj_ref[s]

    @pl.when(j == 0)
    def _():
        m_s[...] = jnp.full(m_s.shape, NEG, F32)
        acc_s[...] = jnp.zeros(acc_s.shape, F32)

    thr = sel_ref[0, 0:1, :]
    has_cut = cut_ref[b * n_q_blocks + i] > 0

    @pl.when(jnp.logical_not(has_cut))
    def _():
        mb_s[...] = jnp.where(st_ref[0] >= thr, 0.0, NEG)

    @pl.when(has_cut)
    def _():
        key_pos = (j * tk + lax.broadcasted_iota(I32, (tk, tq), 0)).astype(F32)
        mb_s[...] = jnp.where(_selected(st_ref[0], key_pos, thr, sel_ref[0, 1:2, :]), 0.0, NEG)

    kind = (i * tq - j * tk - d0_min) // tk
    ones = jnp.ones((ACC_ROWS - HEAD_DIM, tk), BF16)

    def qk(h):
        hs = slice(h * HEAD_DIM, (h + 1) * HEAD_DIM)
        lg_s[h % 2] = jnp.dot(kb_ref[0, :, hs], qt_ref[0, hs, :], preferred_element_type=F32)

    def heads(far):
        qk(0)
        for h in range(N_HEADS_A):
            if h + 1 < N_HEADS_A:
                qk(h + 1)
            lg = lg_s[h % 2] + mb_s[...]
            if far:
                shift = far_ref[h]
            else:
                lg = lg + bias_ref[kind, h]
                shift = 0.0
            m_old = m_s[h:h + 1, :]
            m_new = jnp.maximum(m_old, jnp.max(lg, axis=0, keepdims=True) + shift)
            alpha = jnp.exp(m_old - m_new)
            p = jnp.exp(lg - (m_new - shift)).astype(BF16)
            v_aug = jnp.concatenate([vt_ref[0, h * HEAD_DIM:(h + 1) * HEAD_DIM, :], ones], axis=0)
            rs = slice(h * ACC_ROWS, (h + 1) * ACC_ROWS)
            acc_s[rs, :] = alpha * acc_s[rs, :] + jnp.dot(v_aug, p, preferred_element_type=F32)
            m_s[h:h + 1, :] = m_new

    pl.when(kind >= n_near)(lambda: heads(True))
    pl.when(kind < n_near)(lambda: heads(False))

    @pl.when(j == nj - 1)
    def _():
        for h in range(N_HEADS_A):
            r0 = h * ACC_ROWS
            o_ref[0, h * HEAD_DIM:(h + 1) * HEAD_DIM, :] = (
                acc_s[r0:r0 + HEAD_DIM, :] / acc_s[r0 + HEAD_DIM:r0 + HEAD_DIM + 1, :]).astype(o_ref.dtype)


def _prompt_attention(kb, qt, vt, st, sel, has_cut, bias_tiles, far_bias, n_near, d0_min, tk, tq):
    b, s, d = kb.shape
    ii, jj = _tri_tables(s // tq, tq, tk)
    nj = np.asarray([(i + 1) * tq // tk for i in ii], np.int32)
    grid_spec = pltpu.PrefetchScalarGridSpec(
        num_scalar_prefetch=4,
        grid=(b, len(ii)),
        in_specs=[pl.BlockSpec((1, tk, d), lambda bb, t, ii, jj, nj, hc: (bb, jj[t], 0)),
                  pl.BlockSpec((1, d, tq), lambda bb, t, ii, jj, nj, hc: (bb, 0, ii[t])),
                  pl.BlockSpec((1, d, tk), lambda bb, t, ii, jj, nj, hc: (bb, 0, jj[t])),
                  pl.BlockSpec((1, tk, tq), lambda bb, t, ii, jj, nj, hc: (bb, jj[t], ii[t])),
                  pl.BlockSpec((1, 8, tq), lambda bb, t, ii, jj, nj, hc: (bb, 0, ii[t])),
                  pl.BlockSpec((n_near, N_HEADS_A, tk, tq), lambda bb, t, ii, jj, nj, hc: (0, 0, 0, 0),
                               pipeline_mode=pl.Buffered(1)),
                  pl.BlockSpec(memory_space=pltpu.SMEM)],
        out_specs=pl.BlockSpec((1, d, tq), lambda bb, t, ii, jj, nj, hc: (bb, 0, ii[t])),
        scratch_shapes=[pltpu.VMEM((N_HEADS_A, tq), F32), pltpu.VMEM((N_HEADS_A * ACC_ROWS, tq), F32),
                        pltpu.VMEM((tk, tq), F32), pltpu.VMEM((2, tk, tq), F32)],
    )
    return pl.pallas_call(
        functools.partial(_prompt_attn_kernel, tk=tk, tq=tq, n_near=n_near, d0_min=d0_min,
                          n_q_blocks=s // tq),
        grid_spec=grid_spec,
        out_shape=jax.ShapeDtypeStruct((b, d, s), BF16),
        compiler_params=_params(("parallel", "arbitrary")),
        name="prompt_sparse_attention",
    )(jnp.asarray(ii), jnp.asarray(jj), jnp.asarray(nj), has_cut, kb, qt, vt, st, sel, bias_tiles, far_bias)


PAGES_PER_STEP = 8


def _page_spec(block, u):
    zeros = (0,) * (len(block) - 1)
    return pl.BlockSpec(block, lambda sq, g, pt: (pt[sq, g * PAGES_PER_STEP + u],) + zeros)


def _head_sum(x, t_new):
    acc = x[0:t_new, :]
    for h in range(1, N_IDX_HEADS):
        acc = acc + x[h * t_new:(h + 1) * t_new, :]
    return acc


def _sample_scores_kernel(pt_ref, *refs, t_new):
    pages = refs[:PAGES_PER_STEP]
    qi_ref, w_ref, kinew_ref, o_ref, onew_ref = refs[PAGES_PER_STEP:]
    qi = qi_ref[0]
    w = w_ref[0]

    def score(d):
        return _head_sum(jnp.maximum(d, 0.0) * w, t_new) * INDEX_SCALE

    for u in range(PAGES_PER_STEP):
        o_ref[0, :, u * PAGE_SIZE:(u + 1) * PAGE_SIZE] = score(
            jnp.dot(qi, pages[u][0].astype(BF16), preferred_element_type=F32))

    sc = score(lax.dot_general(qi, kinew_ref[0], (((1,), (1,)), ((), ())), preferred_element_type=F32))
    tq = lax.broadcasted_iota(I32, (t_new, PAGE_SIZE), 0)
    ss = lax.broadcasted_iota(I32, (t_new, PAGE_SIZE), 1)
    onew_ref[0] = jnp.where(ss <= tq, sc, -jnp.inf)


def _sample_scores(page_table, cache_idx, qi_rows, w_rows, ki_new_pad, t_new):
    bd, n_pages = page_table.shape
    rows = N_IDX_HEADS * t_new
    per_seq = lambda shp: pl.BlockSpec((1,) + shp, lambda sq, g, pt: (sq, 0, 0))
    grid_spec = pltpu.PrefetchScalarGridSpec(
        num_scalar_prefetch=1,
        grid=(bd, n_pages // PAGES_PER_STEP),
        in_specs=[_page_spec((1, D_IDX, PAGE_SIZE), u) for u in range(PAGES_PER_STEP)]
        + [per_seq((rows, D_IDX)), per_seq((rows, 1)), per_seq((PAGE_SIZE, D_IDX))],
        out_specs=[pl.BlockSpec((1, t_new, PAGES_PER_STEP * PAGE_SIZE), lambda sq, g, pt: (sq, 0, g)),
                   per_seq((t_new, PAGE_SIZE))],
    )
    return pl.pallas_call(
        functools.partial(_sample_scores_kernel, t_new=t_new),
        grid_spec=grid_spec,
        out_shape=[jax.ShapeDtypeStruct((bd, t_new, n_pages * PAGE_SIZE), F32),
                   jax.ShapeDtypeStruct((bd, t_new, PAGE_SIZE), F32)],
        compiler_params=_params(("parallel", "arbitrary")),
        name="sample_indexer_scores",
    )(page_table, *([cache_idx] * PAGES_PER_STEP), qi_rows, w_rows, ki_new_pad)


def _sample_attn_kernel(pt_ref, *refs, t_new, n_groups, far_pages):
    kp = refs[:PAGES_PER_STEP]
    vp = refs[PAGES_PER_STEP:2 * PAGES_PER_STEP]
    (q_ref, knew_ref, vnew_ref, sc_ref, scnew_ref, sel_ref, bias_ref, o_ref,
     qb_s, m_s, l_s, acc_s) = refs[2 * PAGES_PER_STEP:]
    g = pl.program_id(1)
    rows = N_HEADS_A * t_new
    d = N_HEADS_A * HEAD_DIM

    @pl.when(g == 0)
    def _():
        q_all = jnp.concatenate([q_ref[0].astype(F32)] * N_HEADS_A, axis=0)
        r = lax.broadcasted_iota(I32, (rows, d), 0)
        c = lax.broadcasted_iota(I32, (rows, d), 1)
        qb_s[...] = jnp.where(r // t_new == c // HEAD_DIM, q_all, 0.0).astype(BF16)
        m_s[...] = jnp.full(m_s.shape, NEG, F32)
        l_s[...] = jnp.zeros(l_s.shape, F32)
        acc_s[...] = jnp.zeros(acc_s.shape, F32)

    thr = sel_ref[0, :, 0:1]
    cut = sel_ref[0, :, 1:2]
    qb = qb_s[...]

    contract_last = (((1,), (1,)), ((), ()))

    def logits(qk, score, key_pos0, bias):
        pos = (key_pos0 + lax.broadcasted_iota(I32, (t_new, PAGE_SIZE), 1)).astype(F32)
        mb = jnp.where(_selected(score, pos, thr, cut), 0.0, NEG)
        return qk + bias + jnp.concatenate([mb] * N_HEADS_A, axis=0)

    def update(lg, values_bf, channel_major):
        m_old = m_s[...]
        m_new = jnp.maximum(m_old, jnp.max(lg, axis=1, keepdims=True))
        alpha = jnp.exp(m_old - m_new)
        p = jnp.exp(lg - m_new)
        l_s[...] = alpha * l_s[...] + jnp.sum(p, axis=1, keepdims=True)
        pv = jnp.zeros((rows, d), F32)
        for u, vb in enumerate(values_bf):
            pu = p[:, u * PAGE_SIZE:(u + 1) * PAGE_SIZE].astype(BF16)
            if channel_major:
                pv = pv + lax.dot_general(pu, vb, contract_last, preferred_element_type=F32)
            else:
                pv = pv + jnp.dot(pu, vb, preferred_element_type=F32)
        acc_s[...] = alpha * acc_s[...] + pv
        m_s[...] = m_new

    lgs = []
    for u in range(PAGES_PER_STEP):
        page = g * PAGES_PER_STEP + u
        bias = bias_ref[jnp.where(page < far_pages, 0, 1)]
        qk = jnp.dot(qb, kp[u][0].astype(BF16), preferred_element_type=F32)
        lgs.append(logits(qk, sc_ref[0, :, u * PAGE_SIZE:(u + 1) * PAGE_SIZE], page * PAGE_SIZE, bias))
    update(jnp.concatenate(lgs, axis=1), [vp[u][0].astype(BF16) for u in range(PAGES_PER_STEP)], True)

    @pl.when(g == n_groups - 1)
    def _():
        past = n_groups * PAGES_PER_STEP * PAGE_SIZE
        qk = lax.dot_general(qb, knew_ref[0], contract_last, preferred_element_type=F32)
        update(logits(qk, scnew_ref[0], past, bias_ref[2]), [vnew_ref[0]], False)
        out = []
        for h in range(N_HEADS_A):
            rs = slice(h * t_new, (h + 1) * t_new)
            out.append(acc_s[rs, h * HEAD_DIM:(h + 1) * HEAD_DIM] / l_s[rs, :])
        o_ref[0] = jnp.concatenate(out, axis=1).astype(o_ref.dtype)


def _sample_attention(page_table, cache_k, cache_v, q_s, k_new_pad, v_new_pad, sc_past, sc_new,
                      sel_rows, bias_tiles, t_new):
    bd, n_pages = page_table.shape
    d = cache_k.shape[1]
    rows = N_HEADS_A * t_new
    n_groups = n_pages // PAGES_PER_STEP
    far_pages = n_pages - 1
    assert PAGE_SIZE + 1 >= FAR_DISTANCE
    per_seq = lambda shp: pl.BlockSpec((1,) + shp, lambda sq, g, pt: (sq, 0, 0))
    grid_spec = pltpu.PrefetchScalarGridSpec(
        num_scalar_prefetch=1,
        grid=(bd, n_groups),
        in_specs=[_page_spec((1, d, PAGE_SIZE), u) for u in range(PAGES_PER_STEP)] * 2
        + [per_seq((t_new, d)), per_seq((PAGE_SIZE, d)), per_seq((PAGE_SIZE, d)),
           pl.BlockSpec((1, t_new, PAGES_PER_STEP * PAGE_SIZE), lambda sq, g, pt: (sq, 0, g)),
           per_seq((t_new, PAGE_SIZE)), per_seq((t_new, 128)),
           pl.BlockSpec((3, rows, PAGE_SIZE), lambda sq, g, pt: (0, 0, 0))],
        out_specs=per_seq((t_new, d)),
        scratch_shapes=[pltpu.VMEM((rows, d), BF16), pltpu.VMEM((rows, 1), F32),
                        pltpu.VMEM((rows, 1), F32), pltpu.VMEM((rows, d), F32)],
    )
    return pl.pallas_call(
        functools.partial(_sample_attn_kernel, t_new=t_new, n_groups=n_groups, far_pages=far_pages),
        grid_spec=grid_spec,
        out_shape=jax.ShapeDtypeStruct((bd, t_new, d), BF16),
        compiler_params=_params(("parallel", "arbitrary")),
        name="sample_sparse_attention",
    )(page_table, *([cache_k] * PAGES_PER_STEP), *([cache_v] * PAGES_PER_STEP), q_s, k_new_pad,
      v_new_pad, sc_past, sc_new, sel_rows, bias_tiles)


def _ffn_kernel(*refs, tm, d_ff, tiles_per_seq, fixup):
    if fixup:
        (x_ref, a_ref, r_ref, wo_ref, gf_ref, wup_ref, cw_ref, cb_ref, wdn_ref, gl_ref, x0_ref, x1_ref,
         y_ref, g_ref, gbuf) = refs
    else:
        (x_ref, a_ref, r_ref, wo_ref, gf_ref, wup_ref, cw_ref, cb_ref, wdn_ref, gl_ref,
         y_ref, g_ref, gbuf) = refs
    i = pl.program_id(0)
    d_attn = a_ref.shape[1]
    x1 = (x_ref[...] + jnp.dot(a_ref[...], wo_ref[0:d_attn, :], preferred_element_type=F32)
          + jnp.dot(r_ref[...], wo_ref[d_attn:, :], preferred_element_type=F32))
    xn = _rmsnorm(x1, gf_ref[...]).astype(BF16)
    up = jnp.dot(xn, wup_ref[...], preferred_element_type=F32)
    g = up[:, :d_ff]
    u = up[:, d_ff:]

    @pl.when(i % tiles_per_seq == 0)
    def _():
        gbuf[0:8, :] = jnp.zeros((8, d_ff), F32)

    gbuf[8:8 + tm, :] = g
    prev2 = gbuf[6:6 + tm, :]
    prev1 = gbuf[7:7 + tm, :]
    if fixup:
        t = lax.broadcasted_iota(I32, (tm, d_ff), 0) % 8
        prev2 = jnp.where(t < 2, x0_ref[...], prev2)
        prev1 = jnp.where(t < 1, x1_ref[...], prev1)
    gc = prev2 * cw_ref[0:1, :] + cb_ref[...]
    gc = gc + prev1 * cw_ref[1:2, :]
    gc = gc + g * cw_ref[2:3, :]
    hid = (jax.nn.gelu(gc) * u).astype(BF16)
    x2 = x1 + jnp.dot(hid, wdn_ref[...], preferred_element_type=F32)
    y_ref[...] = _rmsnorm(x2, gl_ref[...])
    if fixup:
        g_ref[...] = g
    else:
        g_ref[0] = g[tm - (FFN_CONV_W - 1):, :]
    gbuf[0:8, :] = gbuf[tm:tm + 8, :]


def _merge_and_ffn(x2d, attn, lru, w_out, norm_ffn, w_up, conv_w, conv_b, w_down, norm_final, tm,
                   rows_per_seq, fix0=None, fix1=None):
    n, d = x2d.shape
    d_ff = w_down.shape[0]
    fixup = fix0 is not None
    tiles_per_seq = max(rows_per_seq // tm, 1)
    row = lambda w: pl.BlockSpec((tm, w), lambda i: (i, 0))
    const = lambda a: pl.BlockSpec(a.shape, lambda i: (0,) * a.ndim, pipeline_mode=pl.Buffered(1))
    cb2 = conv_b.reshape(1, d_ff)
    gf2 = norm_ffn.reshape(1, d)
    gl2 = norm_final.reshape(1, d)
    ins = [x2d, attn, lru, w_out, gf2, w_up, conv_w, cb2, w_down, gl2]
    in_specs = [row(d), row(attn.shape[1]), row(lru.shape[1]), const(w_out), const(gf2), const(w_up),
                const(conv_w), const(cb2), const(w_down), const(gl2)]
    if fixup:
        ins += [fix0, fix1]
        in_specs += [row(d_ff), row(d_ff)]
        g_spec = row(d_ff)
        g_shape = jax.ShapeDtypeStruct((n, d_ff), F32)
    else:
        g_spec = pl.BlockSpec((1, FFN_CONV_W - 1, d_ff), lambda i: (i // tiles_per_seq, 0, 0))
        g_shape = jax.ShapeDtypeStruct((n // rows_per_seq, FFN_CONV_W - 1, d_ff), F32)
    return pl.pallas_call(
        functools.partial(_ffn_kernel, tm=tm, d_ff=d_ff, tiles_per_seq=tiles_per_seq, fixup=fixup),
        grid=(n // tm,),
        in_specs=in_specs,
        out_specs=[row(d), g_spec],
        out_shape=[jax.ShapeDtypeStruct((n, d), F32), g_shape],
        scratch_shapes=[pltpu.VMEM((tm + 8, d_ff), F32)],
        compiler_params=_params(("arbitrary",)),
        name="merge_and_conv_ffn",
    )(*ins)


PROMPT_TILE = 512


def _key_tile(n_keys, unit=64, largest=1024):
    return max(t for t in range(unit, largest + 1, unit) if n_keys % t == 0)


def _block_diag(w):
    h, a, b = w.shape
    eye = jnp.eye(h, dtype=w.dtype)
    return (w[:, :, None, :] * eye[:, None, :, None]).reshape(h * a, h * b)


def kernel(x_prompt, x_sample, cache_k, cache_v, cache_idx_k, state_lru_h, state_lru_conv, state_ffn_conv, page_table, w_in, w_out, norm_mix, norm_ffn, norm_final, rel_bias, lru_conv_w, lru_conv_b, lru_w_a, lru_b_a, lru_w_x, lru_b_x, lru_lambda, ffn_w_up, ffn_conv_w, ffn_conv_b, ffn_w_down):
    bp, s, d = x_prompt.shape
    bd, t_new, _ = x_sample.shape
    depth = w_in.shape[0]
    assert depth == 1 and t_new == 8
    d_lru = lru_lambda.shape[1]
    d_ff = ffn_w_down.shape[1]
    n_pages = page_table.shape[1]
    past = n_pages * PAGE_SIZE
    l = 0

    c_qi_end = 3 * D_ATTN + D_QI
    c_wi_end = c_qi_end + D_IDX + N_IDX_HEADS
    pad_cols = (-(D_IDX + N_IDX_HEADS)) % 128
    w_re = jnp.concatenate([w_in[l][:, :c_qi_end], w_in[l][:, c_wi_end:], w_in[l][:, c_qi_end:c_wi_end],
                            jnp.zeros((d, pad_cols), w_in.dtype)], axis=1).astype(BF16)
    wax = jnp.concatenate([_block_diag(lru_w_a[l]), _block_diag(lru_w_x[l])], axis=1).astype(BF16)
    bax = jnp.concatenate([lru_b_a[l], lru_b_x[l]])
    w_out_b = w_out[l].astype(BF16)
    w_up_b = ffn_w_up[l].astype(BF16)
    w_dn_b = ffn_w_down[l].astype(BF16)

    tk = tq = PROMPT_TILE
    (qb, k, v, kb, vb, qib, ki, kib, wi, lx, lg) = _in_projection(
        x_prompt.reshape(bp * s, d), norm_mix[l], w_re, d_lru, tm=512)
    r3 = lambda a: a.reshape(bp, s, a.shape[-1])
    tr = lambda a: jnp.swapaxes(r3(a), 1, 2)
    st = _prompt_scores(r3(kib), tr(qib), tr(wi), tk, tq)
    sel = _topk_threshold(st, tk, tq, min(TOPK_MAX, s // 4), causal=True)
    has_cut = jnp.any((sel[:, 1, :] < NO_TIE_CUT).reshape(bp * (s // tq), tq), axis=1).astype(I32)
    bias_tiles, n_near, d0_min = _prompt_bias_tiles(rel_bias, tk, tq)
    attn_t = _prompt_attention(r3(kb), tr(qb), tr(vb), st, sel, has_cut, bias_tiles,
                               rel_bias[N_BUCKETS - 1], n_near, d0_min, tk, tq)
    attn_p = jnp.swapaxes(attn_t, 1, 2).reshape(bp * s, D_ATTN)
    lru_p, lconv_p, hlast_p = _lru_branch(
        r3(lx), r3(lg), jnp.zeros((bp, 8, d_lru), F32), jnp.zeros((bp, 1, d_lru), F32),
        lru_conv_w[l], lru_conv_b[l], wax, bax, lru_lambda[l], tc=512)
    y_p, fconv_p = _merge_and_ffn(
        x_prompt.reshape(bp * s, d), attn_p, lru_p.reshape(bp * s, d_lru), w_out_b, norm_ffn[l], w_up_b,
        ffn_conv_w[l], ffn_conv_b[l], w_dn_b, norm_final, tm=256, rows_per_seq=s)
    k_p, v_p, ki_p = k, v, ki

    n_s = bd * t_new
    (qb, k, v, kb, vb, qib, ki, kib, wi, lx, lg) = _in_projection(
        x_sample.reshape(n_s, d), norm_mix[l], w_re, d_lru, tm=min(512, n_s))
    s3 = lambda a: a.reshape(bd, t_new, a.shape[-1])
    qi_rows = s3(qib).reshape(bd, t_new, N_IDX_HEADS, D_IDX).transpose(0, 2, 1, 3).reshape(
        bd, N_IDX_HEADS * t_new, D_IDX)
    w_rows = s3(wi).transpose(0, 2, 1).reshape(bd, N_IDX_HEADS * t_new, 1)
    pad_new = lambda a: jnp.pad(s3(a), ((0, 0), (0, PAGE_SIZE - t_new), (0, 0)))
    idx_pages = jnp.swapaxes(cache_idx_k[l], 1, 2)
    k_pages = cache_k[l].transpose(0, 2, 3, 1).reshape(-1, D_ATTN, PAGE_SIZE)
    v_pages = cache_v[l].transpose(0, 2, 3, 1).reshape(-1, D_ATTN, PAGE_SIZE)
    sc_past, sc_new = _sample_scores(page_table, idx_pages, qi_rows, w_rows, pad_new(kib), t_new)
    l_keys = past + PAGE_SIZE
    sc_all = jnp.concatenate([sc_past, sc_new], axis=2).reshape(n_s, l_keys)
    sel_s = _topk_threshold(sc_all.T[None], _key_tile(l_keys), min(256, n_s),
                            min(TOPK_MAX, (past + t_new) // 4), causal=False)
    sel_rows = jnp.pad(sel_s[0].T.reshape(bd, t_new, 8), ((0, 0), (0, 0), (0, 120)))
    attn_s = _sample_attention(
        page_table, k_pages, v_pages, s3(qb), pad_new(kb), pad_new(vb), sc_past, sc_new, sel_rows,
        _sample_bias_tiles(rel_bias, t_new), t_new)
    buf8 = jnp.pad(state_lru_conv[l], ((0, 0), (8 - (LRU_CONV_W - 1), 0), (0, 0)))
    lru_s, lconv_s, hlast_s = _lru_branch(
        s3(lx), s3(lg), buf8, state_lru_h[l][:, None, :], lru_conv_w[l], lru_conv_b[l], wax, bax,
        lru_lambda[l], tc=t_new)
    fbuf = state_ffn_conv[l]
    fix0 = jnp.pad(fbuf, ((0, 0), (0, t_new - 2), (0, 0))).reshape(n_s, d_ff)
    fix1 = jnp.pad(fbuf[:, 1:2], ((0, 0), (0, t_new - 1), (0, 0))).reshape(n_s, d_ff)
    y_s, g_s = _merge_and_ffn(
        x_sample.reshape(n_s, d), attn_s.reshape(n_s, D_ATTN), lru_s.reshape(n_s, d_lru), w_out_b,
        norm_ffn[l], w_up_b, ffn_conv_w[l], ffn_conv_b[l], w_dn_b, norm_final, tm=min(256, n_s),
        rows_per_seq=n_s, fix0=fix0, fix1=fix1)
    fconv_s = g_s.reshape(bd, t_new, d_ff)[:, t_new - (FFN_CONV_W - 1):, :]

    heads = lambda a, b_, t_: a.reshape(1, b_, t_, N_HEADS_A, HEAD_DIM)
    return (y_p.reshape(bp, s, d), y_s.reshape(bd, t_new, d),
            heads(k_p, bp, s), heads(v_p, bp, s), ki_p.reshape(1, bp, s, D_IDX),
            hlast_p.reshape(1, bp, d_lru), lconv_p[None], fconv_p[None],
            heads(k, bd, t_new), heads(v, bd, t_new), ki.reshape(1, bd, t_new, D_IDX),
            hlast_s.reshape(1, bd, d_lru), lconv_s[None], fconv_s[None])
```

```python
import functools
import math

import jax
import jax.numpy as jnp
import numpy as np
from jax import lax
from jax.experimental import pallas as pl
from jax.experimental.pallas import tpu as pltpu

F32 = jnp.float32
BF16 = jnp.bfloat16
I32 = jnp.int32

HEAD_DIM = 64
N_HEADS_A = 8
D_ATTN = HEAD_DIM * N_HEADS_A
N_IDX_HEADS = 8
D_IDX = 64
D_QI = N_IDX_HEADS * D_IDX
INDEX_SCALE = (N_IDX_HEADS * D_IDX) ** -0.5
TOPK_MAX = 256
N_HEADS_LRU = 8
LRU_CONV_W = 4
LRU_C = 8.0
N_BUCKETS = 32
MAX_DISTANCE = 128
FFN_CONV_W = 3
EPS = 1e-6
PAGE_SIZE = 128

FAR_DISTANCE = 113
NEG = -1e30
LOG2E = math.log2(math.e)
LOGIT_SCALE = HEAD_DIM ** -0.5 * LOG2E
NO_TIE_CUT = float(2 ** 30)

VMEM_LIMIT_BYTES = 56 * 1024 * 1024


def _params(sem):
    return pltpu.CompilerParams(dimension_semantics=sem, vmem_limit_bytes=VMEM_LIMIT_BYTES)


def _rmsnorm(x, g):
    ms = jnp.mean(x * x, axis=-1, keepdims=True)
    return (x * lax.rsqrt(ms + EPS)) * g


def _inproj_kernel(x_ref, g_ref, w_ref, qb_ref, k_ref, v_ref, kb_ref, vb_ref, qib_ref, ki_ref,
                   kib_ref, wi_ref, lx_ref, lg_ref, *, d_lru):
    xn = _rmsnorm(x_ref[...], g_ref[...])
    z = jnp.dot(xn.astype(BF16), w_ref[...], preferred_element_type=F32)
    o = 0
    q = z[:, o:o + D_ATTN]; o += D_ATTN
    k = z[:, o:o + D_ATTN]; o += D_ATTN
    v = z[:, o:o + D_ATTN]; o += D_ATTN
    qi = z[:, o:o + D_QI]; o += D_QI
    lx = z[:, o:o + d_lru]; o += d_lru
    lg = z[:, o:o + d_lru]; o += d_lru
    ki = z[:, o:o + D_IDX]; o += D_IDX
    wi = z[:, o:o + N_IDX_HEADS]
    qb_ref[...] = (q * LOGIT_SCALE).astype(BF16)
    k_ref[...] = k
    v_ref[...] = v
    kb_ref[...] = k.astype(BF16)
    vb_ref[...] = v.astype(BF16)
    qib_ref[...] = qi.astype(BF16)
    ki_ref[...] = ki
    kib_ref[...] = ki.astype(BF16)
    wi_ref[...] = wi
    lx_ref[...] = lx
    lg_ref[...] = lg


def _in_projection(x2d, norm_g, w_re, d_lru, tm):
    n, d = x2d.shape
    nc = w_re.shape[1]
    row = lambda w: pl.BlockSpec((tm, w), lambda i: (i, 0))
    widths = [(D_ATTN, BF16), (D_ATTN, F32), (D_ATTN, F32), (D_ATTN, BF16), (D_ATTN, BF16),
              (D_QI, BF16), (D_IDX, F32), (D_IDX, BF16), (N_IDX_HEADS, F32), (d_lru, F32), (d_lru, F32)]
    return pl.pallas_call(
        functools.partial(_inproj_kernel, d_lru=d_lru),
        grid=(n // tm,),
        in_specs=[row(d), pl.BlockSpec((1, d), lambda i: (0, 0)),
                  pl.BlockSpec((d, nc), lambda i: (0, 0))],
        out_specs=[row(w) for w, _ in widths],
        out_shape=[jax.ShapeDtypeStruct((n, w), dt) for w, dt in widths],
        compiler_params=_params(("parallel",)),
        name="in_projection",
    )(x2d, norm_g.reshape(1, d), w_re)


def _softplus(x):
    return jnp.maximum(x, 0.0) + jnp.log1p(jnp.exp(-jnp.abs(x)))


def _lru_kernel(lx_ref, lg_ref, buf_ref, h0_ref, cw_ref, cb_ref, wax_ref, bax_ref, lam_ref,
                y_ref, newbuf_ref, hlast_ref, xbuf, hcar, a_s, h_s, *, tc, d_lru):
    c = pl.program_id(1)

    @pl.when(c == 0)
    def _():
        xbuf[0:8, :] = buf_ref[0]
        hcar[...] = h0_ref[0]

    xbuf[8:8 + tc, :] = lx_ref[0]
    xc = xbuf[5:5 + tc, :] * cw_ref[0:1, :] + cb_ref[...]
    for j in range(1, LRU_CONV_W):
        xc = xc + xbuf[5 + j:5 + j + tc, :] * cw_ref[j:j + 1, :]
    gates = jnp.dot(xc.astype(BF16), wax_ref[...], preferred_element_type=F32) + bax_ref[...]
    r = jax.nn.sigmoid(gates[:, :d_lru])
    ig = jax.nn.sigmoid(gates[:, d_lru:])
    log_a = -LRU_C * r * _softplus(-lam_ref[...])
    a_s[...] = jnp.exp(log_a)
    neg_expm1 = -jnp.tanh(log_a) * (jnp.exp(2.0 * log_a) + 1.0)
    h_s[...] = jnp.sqrt(neg_expm1) * ig * xc

    def group(gi, h):
        r0 = pl.multiple_of(gi * 8, 8)
        a8 = a_s[pl.ds(r0, 8), :]
        b8 = h_s[pl.ds(r0, 8), :]
        rows = []
        for t in range(8):
            h = a8[t:t + 1, :] * h + b8[t:t + 1, :]
            rows.append(h)
        h_s[pl.ds(r0, 8), :] = jnp.concatenate(rows, axis=0)
        return h

    h = lax.fori_loop(0, tc // 8, group, hcar[...])
    hcar[...] = h
    y_ref[0] = (h_s[...] * jax.nn.gelu(lg_ref[0])).astype(y_ref.dtype)
    newbuf_ref[0] = xbuf[tc + 5:tc + 8, :]
    hlast_ref[0] = h
    xbuf[0:8, :] = xbuf[tc:tc + 8, :]


def _lru_branch(lx, lg, buf8, h0, conv_w, conv_b, wax, bax, lam, tc):
    b, t, c = lx.shape
    seq = lambda: pl.BlockSpec((1, tc, c), lambda i, j: (i, j, 0))
    per_b = lambda r: pl.BlockSpec((1, r, c), lambda i, j: (i, 0, 0))
    const = lambda shp: pl.BlockSpec(shp, lambda i, j: (0, 0))
    return pl.pallas_call(
        functools.partial(_lru_kernel, tc=tc, d_lru=c),
        grid=(b, t // tc),
        in_specs=[seq(), seq(), per_b(8), per_b(1), const((LRU_CONV_W, c)), const((1, c)),
                  const((c, 2 * c)), const((1, 2 * c)), const((1, c))],
        out_specs=[seq(), per_b(LRU_CONV_W - 1), per_b(1)],
        out_shape=[jax.ShapeDtypeStruct((b, t, c), BF16),
                   jax.ShapeDtypeStruct((b, LRU_CONV_W - 1, c), F32),
                   jax.ShapeDtypeStruct((b, 1, c), F32)],
        scratch_shapes=[pltpu.VMEM((tc + 8, c), F32), pltpu.VMEM((1, c), F32),
                        pltpu.VMEM((tc, c), F32), pltpu.VMEM((tc, c), F32)],
        compiler_params=_params(("parallel", "arbitrary")),
        name="rglru_branch",
    )(lx, lg, buf8, h0, conv_w, conv_b.reshape(1, c), wax, bax.reshape(1, 2 * c), lam.reshape(1, c))


def _t5_bucket(dist):
    n = jnp.maximum(dist, 0)
    max_exact = N_BUCKETS // 2
    nf = jnp.maximum(n, 1).astype(F32)
    large = max_exact + (jnp.log(nf / max_exact) / math.log(MAX_DISTANCE / max_exact)
                         * (N_BUCKETS - max_exact)).astype(I32)
    large = jnp.minimum(large, N_BUCKETS - 1)
    return jnp.where(n < max_exact, n, large)


def _bias_lookup(rb_ref, bucket, head):
    acc = jnp.zeros(bucket.shape, F32)
    for b in range(N_BUCKETS):
        acc = jnp.where(bucket == b, rb_ref[b * N_HEADS_A + head], acc)
    return acc


def _bias_tiles_kernel(rb_ref, o_ref, *, tk, tq, d0_min):
    kind = pl.program_id(0)
    head = pl.program_id(1)
    ss = lax.broadcasted_iota(I32, (tk, tq), 0)
    tt = lax.broadcasted_iota(I32, (tk, tq), 1)
    d = d0_min + kind * tk + tt - ss
    val = _bias_lookup(rb_ref, _t5_bucket(d), head)
    o_ref[0, 0] = jnp.where(d < 0, NEG, val * LOG2E)


def _prompt_bias_tiles(rel_bias, tk, tq):
    d0_min = tk - tq
    n_near = -(-(tk - 1 + FAR_DISTANCE - d0_min) // tk)
    tiles = pl.pallas_call(
        functools.partial(_bias_tiles_kernel, tk=tk, tq=tq, d0_min=d0_min),
        grid=(n_near, N_HEADS_A),
        in_specs=[pl.BlockSpec(memory_space=pltpu.SMEM)],
        out_specs=pl.BlockSpec((1, 1, tk, tq), lambda k, h: (k, h, 0, 0)),
        out_shape=jax.ShapeDtypeStruct((n_near, N_HEADS_A, tk, tq), F32),
        compiler_params=_params(("parallel", "parallel")),
        name="prompt_bias_tiles",
    )(rel_bias.reshape(-1))
    return tiles, n_near, d0_min


def _sample_bias_kernel(rb_ref, o_ref, *, t_new):
    kind = pl.program_id(0)
    rows = N_HEADS_A * t_new
    r = lax.broadcasted_iota(I32, (rows, PAGE_SIZE), 0)
    ss = lax.broadcasted_iota(I32, (rows, PAGE_SIZE), 1)
    tq = r % t_new
    head = r // t_new
    base = jnp.where(kind == 0, FAR_DISTANCE + PAGE_SIZE, jnp.where(kind == 1, PAGE_SIZE, 0))
    d = base + tq - ss
    bucket = _t5_bucket(d)
    acc = jnp.zeros((rows, PAGE_SIZE), F32)
    for h in range(N_HEADS_A):
        acc = jnp.where(head == h, _bias_lookup(rb_ref, bucket, h), acc)
    o_ref[0] = jnp.where(d < 0, NEG, acc * LOG2E)


def _sample_bias_tiles(rel_bias, t_new):
    rows = N_HEADS_A * t_new
    return pl.pallas_call(
        functools.partial(_sample_bias_kernel, t_new=t_new),
        grid=(3,),
        in_specs=[pl.BlockSpec(memory_space=pltpu.SMEM)],
        out_specs=pl.BlockSpec((1, rows, PAGE_SIZE), lambda k: (k, 0, 0)),
        out_shape=jax.ShapeDtypeStruct((3, rows, PAGE_SIZE), F32),
        compiler_params=_params(("parallel",)),
        name="sample_bias_tiles",
    )(rel_bias.reshape(-1))


def _tri_tables(n_q_blocks, tq, tk):
    ii, jj = [], []
    for i in range(n_q_blocks):
        for j in range((i + 1) * tq // tk):
            ii.append(i)
            jj.append(j)
    return np.asarray(ii, np.int32), np.asarray(jj, np.int32)


def _scores_kernel(ii_ref, jj_ref, ki_ref, qit_ref, wit_ref, o_ref, *, tk, tq):
    s = pl.program_id(1)
    i = ii_ref[s]
    j = jj_ref[s]
    ki = ki_ref[0]
    acc = jnp.zeros((tk, tq), F32)
    for h in range(N_IDX_HEADS):
        d = jnp.dot(ki, qit_ref[0, h * D_IDX:(h + 1) * D_IDX, :], preferred_element_type=F32)
        acc = acc + jnp.maximum(d, 0.0) * wit_ref[0, h:h + 1, :]
    sc = acc * INDEX_SCALE
    kpos = j * tk + lax.broadcasted_iota(I32, (tk, tq), 0)
    qpos = i * tq + lax.broadcasted_iota(I32, (tk, tq), 1)
    o_ref[0] = jnp.where(kpos <= qpos, sc, -jnp.inf)


def _prompt_scores(kib, qit, wit, tk, tq):
    b, s, _ = kib.shape
    ii, jj = _tri_tables(s // tq, tq, tk)
    grid_spec = pltpu.PrefetchScalarGridSpec(
        num_scalar_prefetch=2,
        grid=(b, len(ii)),
        in_specs=[pl.BlockSpec((1, tk, D_IDX), lambda bb, st, ii, jj: (bb, jj[st], 0)),
                  pl.BlockSpec((1, D_QI, tq), lambda bb, st, ii, jj: (bb, 0, ii[st])),
                  pl.BlockSpec((1, N_IDX_HEADS, tq), lambda bb, st, ii, jj: (bb, 0, ii[st]))],
        out_specs=pl.BlockSpec((1, tk, tq), lambda bb, st, ii, jj: (bb, jj[st], ii[st])),
    )
    return pl.pallas_call(
        functools.partial(_scores_kernel, tk=tk, tq=tq),
        grid_spec=grid_spec,
        out_shape=jax.ShapeDtypeStruct((b, s, s), F32),
        compiler_params=_params(("parallel", "arbitrary")),
        name="prompt_indexer_scores",
    )(jnp.asarray(ii), jnp.asarray(jj), kib, qit, wit)


def _to_key(x):
    bits = lax.bitcast_convert_type(x, I32)
    key = bits ^ ((bits >> 31) & 0x7FFFFFFF)
    return jnp.where(x == 0.0, 0, key)


def _from_key(key):
    bits = key ^ ((key >> 31) & 0x7FFFFFFF)
    return lax.bitcast_convert_type(bits, F32)


INT_MIN = -2 ** 31
INT_MAX = 2 ** 31 - 1
COUNT_ROWS = 32


def _bisect_kth(count_ge, lo, hi, cnt_lo, cnt_hi, k_sel):
    def n_active(lo, hi, cnt_lo, cnt_hi):
        return jnp.max((((cnt_lo - cnt_hi) > 2) & (lo + 1 != hi)).astype(I32))

    def body(c):
        lo, hi, cnt_lo, cnt_hi, _ = c
        act = ((cnt_lo - cnt_hi) > 2) & (lo + 1 != hi)
        mid = _to_key(0.5 * _from_key(lo) + 0.5 * _from_key(hi))
        mid = jnp.where((lo < 0) & (hi > 0), 0, mid)
        mid = jnp.where((lo == 0) & (hi > 1), 1, mid)
        inside = (mid > lo) & (mid < hi)
        mid = jnp.where(inside, mid, (lo >> 1) + (hi >> 1) + (lo & hi & 1))
        cnt = count_ge(mid)
        up = act & (cnt >= k_sel)
        dn = act & (cnt < k_sel)
        lo = jnp.where(up, mid, lo)
        cnt_lo = jnp.where(up, cnt, cnt_lo)
        hi = jnp.where(dn, mid, hi)
        cnt_hi = jnp.where(dn, cnt, cnt_hi)
        return lo, hi, cnt_lo, cnt_hi, n_active(lo, hi, cnt_lo, cnt_hi)

    out = lax.while_loop(lambda c: c[4] > 0, body,
                         (lo, hi, cnt_lo, cnt_hi, n_active(lo, hi, cnt_lo, cnt_hi)))
    return out[:4]


def _threshold_kernel(ii_ref, jj_ref, nj_ref, st_ref, o_ref, slab, gmax, *, tk, tq, k_sel, rows, pos_bits):
    s = pl.program_id(1)
    j = jj_ref[s]
    nj = nj_ref[s]
    key = _to_key(st_ref[0])
    slab[j] = key

    @pl.when(j == 0)
    def _():
        gmax[...] = key

    @pl.when(j > 0)
    def _():
        gmax[...] = jnp.maximum(gmax[...], key)

    def reduce_rows(read, n_blocks, init, step, final):
        def body(jb, acc):
            for rc in range(tk // rows):
                acc = step(acc, read(jb, rc), jb, rc)
            return acc
        return final(lax.fori_loop(0, n_blocks, body, jnp.full((rows, tq), init, I32)))

    sum_rows = lambda a: jnp.sum(a, axis=0, keepdims=True)
    read_slab = lambda jb, rc: slab[jb, rc * rows:(rc + 1) * rows, :]
    read_gmax = lambda jb, rc: gmax[rc * rows:(rc + 1) * rows, :]

    def count(read, n_blocks, pred):
        return reduce_rows(read, n_blocks, 0, lambda acc, blk, jb, rc: acc + jnp.where(pred(blk, jb, rc), 1, 0),
                           sum_rows)

    @pl.when(j == nj - 1)
    def _():
        g = gmax[...]
        top = jnp.max(g, axis=0, keepdims=True) + 1
        zero = jnp.zeros((1, tq), I32)
        if tk >= k_sel:
            g_lo = jnp.min(g, axis=0, keepdims=True)
            lo, _, _, _ = _bisect_kth(lambda m: count(read_gmax, 1, lambda blk, jb, rc: blk >= m),
                                      g_lo, top, jnp.full((1, tq), tk, I32), zero, k_sel)
        else:
            lo = jnp.full((1, tq), INT_MIN, I32)
        count_ge = lambda m: count(read_slab, nj, lambda blk, jb, rc: blk >= m)
        lo, hi, cnt_lo, cnt_hi = _bisect_kth(count_ge, lo, top, count_ge(lo), zero, k_sel)
        mn = reduce_rows(read_slab, nj, INT_MAX,
                         lambda acc, blk, jb, rc: jnp.minimum(acc, jnp.where(blk >= lo, blk, INT_MAX)),
                         lambda a: jnp.min(a, axis=0, keepdims=True))
        mx = reduce_rows(read_slab, nj, INT_MIN,
                         lambda acc, blk, jb, rc: jnp.maximum(acc, jnp.where(blk < hi, blk, INT_MIN)),
                         lambda a: jnp.max(a, axis=0, keepdims=True))
        unit = lo + 1 == hi
        n_in = cnt_lo - cnt_hi
        rank = k_sel - cnt_hi
        kth = jnp.where(unit, lo, jnp.where(rank == 1, mx, mn))
        n_gt = jnp.where(unit | (rank == 1) | (mx == mn), cnt_hi, cnt_hi + 1)
        n_eq = jnp.where(unit, n_in, jnp.where((n_in == 2) & (mx == mn), 2, 1))
        need = k_sel - n_gt
        neg_inf_key = _to_key(jnp.full((1, tq), -jnp.inf, F32))
        tie_cut = (n_eq > need) & (kth != neg_inf_key)
        o_ref[0, 0:1, :] = _from_key(kth)
        o_ref[0, 1:2, :] = jnp.full((1, tq), NO_TIE_CUT, F32)
        o_ref[0, 2:8, :] = jnp.zeros((6, tq), F32)

        @pl.when(jnp.max(tie_cut.astype(I32)) > 0)
        def _():
            row = lax.broadcasted_iota(I32, (rows, tq), 0)

            def pos_step(it, cut):
                cand = cut | (jnp.int32(1) << (pos_bits - 1 - it))
                c = count(read_slab, nj,
                          lambda blk, jb, rc: (blk == kth) & (jb * tk + rc * rows + row < cand))
                return jnp.where(c < need, cand, cut)

            cut = lax.fori_loop(0, pos_bits, pos_step, jnp.zeros((1, tq), I32))
            o_ref[0, 1:2, :] = jnp.where(tie_cut, cut.astype(F32), NO_TIE_CUT)


def _topk_threshold(st, tk, tq, k_sel, causal):
    b, l, n = st.shape
    if causal:
        ii, jj = _tri_tables(n // tq, tq, tk)
    else:
        nb = l // tk
        ii = np.repeat(np.arange(n // tq, dtype=np.int32), nb)
        jj = np.tile(np.arange(nb, dtype=np.int32), n // tq)
    nj = np.asarray([(i + 1) * tq // tk if causal else l // tk for i in ii], np.int32)
    rows = COUNT_ROWS
    assert tk % rows == 0
    grid_spec = pltpu.PrefetchScalarGridSpec(
        num_scalar_prefetch=3,
        grid=(b, len(ii)),
        in_specs=[pl.BlockSpec((1, tk, tq), lambda bb, st_, ii, jj, nj: (bb, jj[st_], ii[st_]))],
        out_specs=pl.BlockSpec((1, 8, tq), lambda bb, st_, ii, jj, nj: (bb, 0, ii[st_])),
        scratch_shapes=[pltpu.VMEM((int(nj.max()), tk, tq), I32), pltpu.VMEM((tk, tq), I32)],
    )
    return pl.pallas_call(
        functools.partial(_threshold_kernel, tk=tk, tq=tq, k_sel=k_sel, rows=rows,
                          pos_bits=max(1, (l - 1).bit_length())),
        grid_spec=grid_spec,
        out_shape=jax.ShapeDtypeStruct((b, 8, n), F32),
        compiler_params=_params(("parallel", "arbitrary")),
        name="topk_threshold",
    )(jnp.asarray(ii), jnp.asarray(jj), jnp.asarray(nj), st)


def _selected(score, key_pos, thr, cut):
    return (score > thr) | ((score == thr) & (key_pos <= cut))


ACC_ROWS = HEAD_DIM + 16


def _prompt_attn_kernel(ii_ref, jj_ref, nj_ref, cut_ref, kb_ref, qt_ref, vt_ref, st_ref, sel_ref, bias_ref,
                        far_ref, o_ref, m_s, acc_s, mb_s, lg_s, *, tk, tq, n_near, d0_min, n_q_blocks):
    b = pl.program_id(0)
    s = pl.program_id(1)
    i = ii_ref[s]
    j = jj_ref[s]
    nj = nj_ref[s]

    @pl.when(j == 0)
    def _():
        m_s[...] = jnp.full(m_s.shape, NEG, F32)
        acc_s[...] = jnp.zeros(acc_s.shape, F32)

    thr = sel_ref[0, 0:1, :]
    has_cut = cut_ref[b * n_q_blocks + i] > 0

    @pl.when(jnp.logical_not(has_cut))
    def _():
        mb_s[...] = jnp.where(st_ref[0] >= thr, 0.0, NEG)

    @pl.when(has_cut)
    def _():
        key_pos = (j * tk + lax.broadcasted_iota(I32, (tk, tq), 0)).astype(F32)
        mb_s[...] = jnp.where(_selected(st_ref[0], key_pos, thr, sel_ref[0, 1:2, :]), 0.0, NEG)

    kind = (i * tq - j * tk - d0_min) // tk
    ones = jnp.ones((ACC_ROWS - HEAD_DIM, tk), BF16)

    def qk(h):
        hs = slice(h * HEAD_DIM, (h + 1) * HEAD_DIM)
        lg_s[h % 2] = jnp.dot(kb_ref[0, :, hs], qt_ref[0, hs, :], preferred_element_type=F32)

    def heads(far):
        qk(0)
        for h in range(N_HEADS_A):
            if h + 1 < N_HEADS_A:
                qk(h + 1)
            lg = lg_s[h % 2] + mb_s[...]
            if far:
                shift = far_ref[h] * LOG2E
            else:
                lg = lg + bias_ref[kind, h]
                shift = 0.0
            m_old = m_s[h:h + 1, :]
            m_new = jnp.maximum(m_old, jnp.max(lg, axis=0, keepdims=True) + shift)
            alpha = jnp.exp2(m_old - m_new)
            p = jnp.exp2(lg - (m_new - shift)).astype(BF16)
            v_aug = jnp.concatenate([vt_ref[0, h * HEAD_DIM:(h + 1) * HEAD_DIM, :], ones], axis=0)
            rs = slice(h * ACC_ROWS, (h + 1) * ACC_ROWS)
            acc_s[rs, :] = alpha * acc_s[rs, :] + jnp.dot(v_aug, p, preferred_element_type=F32)
            m_s[h:h + 1, :] = m_new

    pl.when(kind >= n_near)(lambda: heads(True))
    pl.when(kind < n_near)(lambda: heads(False))

    @pl.when(j == nj - 1)
    def _():
        for h in range(N_HEADS_A):
            r0 = h * ACC_ROWS
            o_ref[0, h * HEAD_DIM:(h + 1) * HEAD_DIM, :] = (
                acc_s[r0:r0 + HEAD_DIM, :] / acc_s[r0 + HEAD_DIM:r0 + HEAD_DIM + 1, :]).astype(o_ref.dtype)


def _prompt_attention(kb, qt, vt, st, sel, has_cut, bias_tiles, far_bias, n_near, d0_min, tk, tq):
    b, s, d = kb.shape
    ii, jj = _tri_tables(s // tq, tq, tk)
    nj = np.asarray([(i + 1) * tq // tk for i in ii], np.int32)
    grid_spec = pltpu.PrefetchScalarGridSpec(
        num_scalar_prefetch=4,
        grid=(b, len(ii)),
        in_specs=[pl.BlockSpec((1, tk, d), lambda bb, t, ii, jj, nj, hc: (bb, jj[t], 0)),
                  pl.BlockSpec((1, d, tq), lambda bb, t, ii, jj, nj, hc: (bb, 0, ii[t])),
                  pl.BlockSpec((1, d, tk), lambda bb, t, ii, jj, nj, hc: (bb, 0, jj[t])),
                  pl.BlockSpec((1, tk, tq), lambda bb, t, ii, jj, nj, hc: (bb, jj[t], ii[t])),
                  pl.BlockSpec((1, 8, tq), lambda bb, t, ii, jj, nj, hc: (bb, 0, ii[t])),
                  pl.BlockSpec((n_near, N_HEADS_A, tk, tq), lambda bb, t, ii, jj, nj, hc: (0, 0, 0, 0),
                               pipeline_mode=pl.Buffered(1)),
                  pl.BlockSpec(memory_space=pltpu.SMEM)],
        out_specs=pl.BlockSpec((1, d, tq), lambda bb, t, ii, jj, nj, hc: (bb, 0, ii[t])),
        scratch_shapes=[pltpu.VMEM((N_HEADS_A, tq), F32), pltpu.VMEM((N_HEADS_A * ACC_ROWS, tq), F32),
                        pltpu.VMEM((tk, tq), F32), pltpu.VMEM((2, tk, tq), F32)],
    )
    return pl.pallas_call(
        functools.partial(_prompt_attn_kernel, tk=tk, tq=tq, n_near=n_near, d0_min=d0_min,
                          n_q_blocks=s // tq),
        grid_spec=grid_spec,
        out_shape=jax.ShapeDtypeStruct((b, d, s), BF16),
        compiler_params=_params(("parallel", "arbitrary")),
        name="prompt_sparse_attention",
    )(jnp.asarray(ii), jnp.asarray(jj), jnp.asarray(nj), has_cut, kb, qt, vt, st, sel, bias_tiles, far_bias)


PAGES_PER_STEP = 16


def _page_spec(block, u):
    zeros = (0,) * (len(block) - 1)
    return pl.BlockSpec(block, lambda sq, g, pt: (pt[sq, g * PAGES_PER_STEP + u],) + zeros)


def _head_sum(x, t_new):
    acc = x[0:t_new, :]
    for h in range(1, N_IDX_HEADS):
        acc = acc + x[h * t_new:(h + 1) * t_new, :]
    return acc


def _sample_scores_kernel(pt_ref, *refs, t_new):
    pages = refs[:PAGES_PER_STEP]
    qi_ref, w_ref, kinew_ref, o_ref, onew_ref = refs[PAGES_PER_STEP:]
    qi = qi_ref[0]
    w = w_ref[0]

    def score(d):
        return _head_sum(jnp.maximum(d, 0.0) * w, t_new) * INDEX_SCALE

    for u in range(PAGES_PER_STEP):
        o_ref[0, :, u * PAGE_SIZE:(u + 1) * PAGE_SIZE] = score(
            jnp.dot(qi, pages[u][0].astype(BF16), preferred_element_type=F32))

    sc = score(lax.dot_general(qi, kinew_ref[0], (((1,), (1,)), ((), ())), preferred_element_type=F32))
    tq = lax.broadcasted_iota(I32, (t_new, PAGE_SIZE), 0)
    ss = lax.broadcasted_iota(I32, (t_new, PAGE_SIZE), 1)
    onew_ref[0] = jnp.where(ss <= tq, sc, -jnp.inf)


def _sample_scores(page_table, cache_idx, qi_rows, w_rows, ki_new_pad, t_new):
    bd, n_pages = page_table.shape
    rows = N_IDX_HEADS * t_new
    per_seq = lambda shp: pl.BlockSpec((1,) + shp, lambda sq, g, pt: (sq, 0, 0))
    grid_spec = pltpu.PrefetchScalarGridSpec(
        num_scalar_prefetch=1,
        grid=(bd, n_pages // PAGES_PER_STEP),
        in_specs=[_page_spec((1, D_IDX, PAGE_SIZE), u) for u in range(PAGES_PER_STEP)]
        + [per_seq((rows, D_IDX)), per_seq((rows, 1)), per_seq((PAGE_SIZE, D_IDX))],
        out_specs=[pl.BlockSpec((1, t_new, PAGES_PER_STEP * PAGE_SIZE), lambda sq, g, pt: (sq, 0, g)),
                   per_seq((t_new, PAGE_SIZE))],
    )
    return pl.pallas_call(
        functools.partial(_sample_scores_kernel, t_new=t_new),
        grid_spec=grid_spec,
        out_shape=[jax.ShapeDtypeStruct((bd, t_new, n_pages * PAGE_SIZE), F32),
                   jax.ShapeDtypeStruct((bd, t_new, PAGE_SIZE), F32)],
        compiler_params=_params(("parallel", "arbitrary")),
        name="sample_indexer_scores",
    )(page_table, *([cache_idx] * PAGES_PER_STEP), qi_rows, w_rows, ki_new_pad)


def _sample_attn_kernel(pt_ref, *refs, t_new, n_groups, far_pages):
    kp = refs[:PAGES_PER_STEP]
    vp = refs[PAGES_PER_STEP:2 * PAGES_PER_STEP]
    (q_ref, knew_ref, vnew_ref, sc_ref, scnew_ref, sel_ref, bias_ref, o_ref,
     qb_s, m_s, l_s, acc_s) = refs[2 * PAGES_PER_STEP:]
    g = pl.program_id(1)
    rows = N_HEADS_A * t_new
    d = N_HEADS_A * HEAD_DIM

    @pl.when(g == 0)
    def _():
        q_all = jnp.concatenate([q_ref[0].astype(F32)] * N_HEADS_A, axis=0)
        r = lax.broadcasted_iota(I32, (rows, d), 0)
        c = lax.broadcasted_iota(I32, (rows, d), 1)
        qb_s[...] = jnp.where(r // t_new == c // HEAD_DIM, q_all, 0.0).astype(BF16)
        m_s[...] = jnp.full(m_s.shape, NEG, F32)
        l_s[...] = jnp.zeros(l_s.shape, F32)
        acc_s[...] = jnp.zeros(acc_s.shape, F32)

    thr = sel_ref[0, :, 0:1]
    cut = sel_ref[0, :, 1:2]
    qb = qb_s[...]

    contract_last = (((1,), (1,)), ((), ()))

    def logits(qk, score, key_pos0, bias):
        pos = (key_pos0 + lax.broadcasted_iota(I32, (t_new, PAGE_SIZE), 1)).astype(F32)
        mb = jnp.where(_selected(score, pos, thr, cut), 0.0, NEG)
        return qk + bias + jnp.concatenate([mb] * N_HEADS_A, axis=0)

    def update(lg, values_bf, channel_major):
        m_old = m_s[...]
        m_new = jnp.maximum(m_old, jnp.max(lg, axis=1, keepdims=True))
        alpha = jnp.exp2(m_old - m_new)
        p = jnp.exp2(lg - m_new)
        l_s[...] = alpha * l_s[...] + jnp.sum(p, axis=1, keepdims=True)
        pv = jnp.zeros((rows, d), F32)
        for u, vb in enumerate(values_bf):
            pu = p[:, u * PAGE_SIZE:(u + 1) * PAGE_SIZE].astype(BF16)
            if channel_major:
                pv = pv + lax.dot_general(pu, vb, contract_last, preferred_element_type=F32)
            else:
                pv = pv + jnp.dot(pu, vb, preferred_element_type=F32)
        acc_s[...] = alpha * acc_s[...] + pv
        m_s[...] = m_new

    lgs = []
    for u in range(PAGES_PER_STEP):
        page = g * PAGES_PER_STEP + u
        bias = bias_ref[jnp.where(page < far_pages, 0, 1)]
        qk = jnp.dot(qb, kp[u][0].astype(BF16), preferred_element_type=F32)
        lgs.append(logits(qk, sc_ref[0, :, u * PAGE_SIZE:(u + 1) * PAGE_SIZE], page * PAGE_SIZE, bias))
    update(jnp.concatenate(lgs, axis=1), [vp[u][0].astype(BF16) for u in range(PAGES_PER_STEP)], True)

    @pl.when(g == n_groups - 1)
    def _():
        past = n_groups * PAGES_PER_STEP * PAGE_SIZE
        qk = lax.dot_general(qb, knew_ref[0], contract_last, preferred_element_type=F32)
        update(logits(qk, scnew_ref[0], past, bias_ref[2]), [vnew_ref[0]], False)
        out = []
        for h in range(N_HEADS_A):
            rs = slice(h * t_new, (h + 1) * t_new)
            out.append(acc_s[rs, h * HEAD_DIM:(h + 1) * HEAD_DIM] / l_s[rs, :])
        o_ref[0] = jnp.concatenate(out, axis=1).astype(o_ref.dtype)


def _sample_attention(page_table, cache_k, cache_v, q_s, k_new_pad, v_new_pad, sc_past, sc_new,
                      sel_rows, bias_tiles, t_new):
    bd, n_pages = page_table.shape
    d = cache_k.shape[1]
    rows = N_HEADS_A * t_new
    n_groups = n_pages // PAGES_PER_STEP
    far_pages = n_pages - 1
    assert PAGE_SIZE + 1 >= FAR_DISTANCE
    per_seq = lambda shp: pl.BlockSpec((1,) + shp, lambda sq, g, pt: (sq, 0, 0))
    grid_spec = pltpu.PrefetchScalarGridSpec(
        num_scalar_prefetch=1,
        grid=(bd, n_groups),
        in_specs=[_page_spec((1, d, PAGE_SIZE), u) for u in range(PAGES_PER_STEP)] * 2
        + [per_seq((t_new, d)), per_seq((PAGE_SIZE, d)), per_seq((PAGE_SIZE, d)),
           pl.BlockSpec((1, t_new, PAGES_PER_STEP * PAGE_SIZE), lambda sq, g, pt: (sq, 0, g)),
           per_seq((t_new, PAGE_SIZE)), per_seq((t_new, 128)),
           pl.BlockSpec((3, rows, PAGE_SIZE), lambda sq, g, pt: (0, 0, 0))],
        out_specs=per_seq((t_new, d)),
        scratch_shapes=[pltpu.VMEM((rows, d), BF16), pltpu.VMEM((rows, 1), F32),
                        pltpu.VMEM((rows, 1), F32), pltpu.VMEM((rows, d), F32)],
    )
    return pl.pallas_call(
        functools.partial(_sample_attn_kernel, t_new=t_new, n_groups=n_groups, far_pages=far_pages),
        grid_spec=grid_spec,
        out_shape=jax.ShapeDtypeStruct((bd, t_new, d), BF16),
        compiler_params=_params(("parallel", "arbitrary")),
        name="sample_sparse_attention",
    )(page_table, *([cache_k] * PAGES_PER_STEP), *([cache_v] * PAGES_PER_STEP), q_s, k_new_pad,
      v_new_pad, sc_past, sc_new, sel_rows, bias_tiles)


def _ffn_kernel(*refs, tm, d_ff, tiles_per_seq, fixup):
    if fixup:
        (x_ref, a_ref, r_ref, wo_ref, gf_ref, wup_ref, cw_ref, cb_ref, wdn_ref, gl_ref, x0_ref, x1_ref,
         y_ref, g_ref, gbuf) = refs
    else:
        (x_ref, a_ref, r_ref, wo_ref, gf_ref, wup_ref, cw_ref, cb_ref, wdn_ref, gl_ref,
         y_ref, g_ref, gbuf) = refs
    i = pl.program_id(0)
    d_attn = a_ref.shape[1]
    x1 = (x_ref[...] + jnp.dot(a_ref[...], wo_ref[0:d_attn, :], preferred_element_type=F32)
          + jnp.dot(r_ref[...], wo_ref[d_attn:, :], preferred_element_type=F32))
    xn = _rmsnorm(x1, gf_ref[...]).astype(BF16)
    up = jnp.dot(xn, wup_ref[...], preferred_element_type=F32)
    g = up[:, :d_ff]
    u = up[:, d_ff:]

    @pl.when(i % tiles_per_seq == 0)
    def _():
        gbuf[0:8, :] = jnp.zeros((8, d_ff), F32)

    gbuf[8:8 + tm, :] = g
    prev2 = gbuf[6:6 + tm, :]
    prev1 = gbuf[7:7 + tm, :]
    if fixup:
        t = lax.broadcasted_iota(I32, (tm, d_ff), 0) % 8
        prev2 = jnp.where(t < 2, x0_ref[...], prev2)
        prev1 = jnp.where(t < 1, x1_ref[...], prev1)
    gc = prev2 * cw_ref[0:1, :] + cb_ref[...]
    gc = gc + prev1 * cw_ref[1:2, :]
    gc = gc + g * cw_ref[2:3, :]
    hid = (jax.nn.gelu(gc) * u).astype(BF16)
    x2 = x1 + jnp.dot(hid, wdn_ref[...], preferred_element_type=F32)
    y_ref[...] = _rmsnorm(x2, gl_ref[...])
    if fixup:
        g_ref[...] = g
    else:
        g_ref[0] = g[tm - (FFN_CONV_W - 1):, :]
    gbuf[0:8, :] = gbuf[tm:tm + 8, :]


def _merge_and_ffn(x2d, attn, lru, w_out, norm_ffn, w_up, conv_w, conv_b, w_down, norm_final, tm,
                   rows_per_seq, fix0=None, fix1=None):
    n, d = x2d.shape
    d_ff = w_down.shape[0]
    fixup = fix0 is not None
    tiles_per_seq = max(rows_per_seq // tm, 1)
    row = lambda w: pl.BlockSpec((tm, w), lambda i: (i, 0))
    const = lambda a: pl.BlockSpec(a.shape, lambda i: (0,) * a.ndim, pipeline_mode=pl.Buffered(1))
    cb2 = conv_b.reshape(1, d_ff)
    gf2 = norm_ffn.reshape(1, d)
    gl2 = norm_final.reshape(1, d)
    ins = [x2d, attn, lru, w_out, gf2, w_up, conv_w, cb2, w_down, gl2]
    in_specs = [row(d), row(attn.shape[1]), row(lru.shape[1]), const(w_out), const(gf2), const(w_up),
                const(conv_w), const(cb2), const(w_down), const(gl2)]
    if fixup:
        ins += [fix0, fix1]
        in_specs += [row(d_ff), row(d_ff)]
        g_spec = row(d_ff)
        g_shape = jax.ShapeDtypeStruct((n, d_ff), F32)
    else:
        g_spec = pl.BlockSpec((1, FFN_CONV_W - 1, d_ff), lambda i: (i // tiles_per_seq, 0, 0))
        g_shape = jax.ShapeDtypeStruct((n // rows_per_seq, FFN_CONV_W - 1, d_ff), F32)
    return pl.pallas_call(
        functools.partial(_ffn_kernel, tm=tm, d_ff=d_ff, tiles_per_seq=tiles_per_seq, fixup=fixup),
        grid=(n // tm,),
        in_specs=in_specs,
        out_specs=[row(d), g_spec],
        out_shape=[jax.ShapeDtypeStruct((n, d), F32), g_shape],
        scratch_shapes=[pltpu.VMEM((tm + 8, d_ff), F32)],
        compiler_params=_params(("arbitrary",)),
        name="merge_and_conv_ffn",
    )(*ins)


PROMPT_TILE = 512


def _key_tile(n_keys, unit=64, largest=1024):
    return max(t for t in range(unit, largest + 1, unit) if n_keys % t == 0)


def _block_diag(w):
    h, a, b = w.shape
    eye = jnp.eye(h, dtype=w.dtype)
    return (w[:, :, None, :] * eye[:, None, :, None]).reshape(h * a, h * b)


def kernel(x_prompt, x_sample, cache_k, cache_v, cache_idx_k, state_lru_h, state_lru_conv, state_ffn_conv, page_table, w_in, w_out, norm_mix, norm_ffn, norm_final, rel_bias, lru_conv_w, lru_conv_b, lru_w_a, lru_b_a, lru_w_x, lru_b_x, lru_lambda, ffn_w_up, ffn_conv_w, ffn_conv_b, ffn_w_down):
    bp, s, d = x_prompt.shape
    bd, t_new, _ = x_sample.shape
    depth = w_in.shape[0]
    assert depth == 1 and t_new == 8
    d_lru = lru_lambda.shape[1]
    d_ff = ffn_w_down.shape[1]
    n_pages = page_table.shape[1]
    past = n_pages * PAGE_SIZE
    l = 0

    c_qi_end = 3 * D_ATTN + D_QI
    c_wi_end = c_qi_end + D_IDX + N_IDX_HEADS
    pad_cols = (-(D_IDX + N_IDX_HEADS)) % 128
    w_re = jnp.concatenate([w_in[l][:, :c_qi_end], w_in[l][:, c_wi_end:], w_in[l][:, c_qi_end:c_wi_end],
                            jnp.zeros((d, pad_cols), w_in.dtype)], axis=1).astype(BF16)
    wax = jnp.concatenate([_block_diag(lru_w_a[l]), _block_diag(lru_w_x[l])], axis=1).astype(BF16)
    bax = jnp.concatenate([lru_b_a[l], lru_b_x[l]])
    w_out_b = w_out[l].astype(BF16)
    w_up_b = ffn_w_up[l].astype(BF16)
    w_dn_b = ffn_w_down[l].astype(BF16)

    tk = tq = PROMPT_TILE
    (qb, k, v, kb, vb, qib, ki, kib, wi, lx, lg) = _in_projection(
        x_prompt.reshape(bp * s, d), norm_mix[l], w_re, d_lru, tm=512)
    r3 = lambda a: a.reshape(bp, s, a.shape[-1])
    tr = lambda a: jnp.swapaxes(r3(a), 1, 2)
    st = _prompt_scores(r3(kib), tr(qib), tr(wi), tk, tq)
    sel = _topk_threshold(st, tk, tq, min(TOPK_MAX, s // 4), causal=True)
    has_cut = jnp.any((sel[:, 1, :] < NO_TIE_CUT).reshape(bp * (s // tq), tq), axis=1).astype(I32)
    bias_tiles, n_near, d0_min = _prompt_bias_tiles(rel_bias, tk, tq)
    attn_t = _prompt_attention(r3(kb), tr(qb), tr(vb), st, sel, has_cut, bias_tiles,
                               rel_bias[N_BUCKETS - 1], n_near, d0_min, tk, tq)
    attn_p = jnp.swapaxes(attn_t, 1, 2).reshape(bp * s, D_ATTN)
    lru_p, lconv_p, hlast_p = _lru_branch(
        r3(lx), r3(lg), jnp.zeros((bp, 8, d_lru), F32), jnp.zeros((bp, 1, d_lru), F32),
        lru_conv_w[l], lru_conv_b[l], wax, bax, lru_lambda[l], tc=512)
    y_p, fconv_p = _merge_and_ffn(
        x_prompt.reshape(bp * s, d), attn_p, lru_p.reshape(bp * s, d_lru), w_out_b, norm_ffn[l], w_up_b,
        ffn_conv_w[l], ffn_conv_b[l], w_dn_b, norm_final, tm=256, rows_per_seq=s)
    k_p, v_p, ki_p = k, v, ki

    n_s = bd * t_new
    (qb, k, v, kb, vb, qib, ki, kib, wi, lx, lg) = _in_projection(
        x_sample.reshape(n_s, d), norm_mix[l], w_re, d_lru, tm=min(512, n_s))
    s3 = lambda a: a.reshape(bd, t_new, a.shape[-1])
    qi_rows = s3(qib).reshape(bd, t_new, N_IDX_HEADS, D_IDX).transpose(0, 2, 1, 3).reshape(
        bd, N_IDX_HEADS * t_new, D_IDX)
    w_rows = s3(wi).transpose(0, 2, 1).reshape(bd, N_IDX_HEADS * t_new, 1)
    pad_new = lambda a: jnp.pad(s3(a), ((0, 0), (0, PAGE_SIZE - t_new), (0, 0)))
    idx_pages = jnp.swapaxes(cache_idx_k[l], 1, 2)
    k_pages = cache_k[l].transpose(0, 2, 3, 1).reshape(-1, D_ATTN, PAGE_SIZE)
    v_pages = cache_v[l].transpose(0, 2, 3, 1).reshape(-1, D_ATTN, PAGE_SIZE)
    sc_past, sc_new = _sample_scores(page_table, idx_pages, qi_rows, w_rows, pad_new(kib), t_new)
    l_keys = past + PAGE_SIZE
    sc_all = jnp.concatenate([sc_past, sc_new], axis=2).reshape(n_s, l_keys)
    sel_s = _topk_threshold(sc_all.T[None], _key_tile(l_keys), min(256, n_s),
                            min(TOPK_MAX, (past + t_new) // 4), causal=False)
    sel_rows = jnp.pad(sel_s[0].T.reshape(bd, t_new, 8), ((0, 0), (0, 0), (0, 120)))
    attn_s = _sample_attention(
        page_table, k_pages, v_pages, s3(qb), pad_new(kb), pad_new(vb), sc_past, sc_new, sel_rows,
        _sample_bias_tiles(rel_bias, t_new), t_new)
    buf8 = jnp.pad(state_lru_conv[l], ((0, 0), (8 - (LRU_CONV_W - 1), 0), (0, 0)))
    lru_s, lconv_s, hlast_s = _lru_branch(
        s3(lx), s3(lg), buf8, state_lru_h[l][:, None, :], lru_conv_w[l], lru_conv_b[l], wax, bax,
        lru_lambda[l], tc=t_new)
    fbuf = state_ffn_conv[l]
    fix0 = jnp.pad(fbuf, ((0, 0), (0, t_new - 2), (0, 0))).reshape(n_s, d_ff)
    fix1 = jnp.pad(fbuf[:, 1:2], ((0, 0), (0, t_new - 1), (0, 0))).reshape(n_s, d_ff)
    y_s, g_s = _merge_and_ffn(
        x_sample.reshape(n_s, d), attn_s.reshape(n_s, D_ATTN), lru_s.reshape(n_s, d_lru), w_out_b,
        norm_ffn[l], w_up_b, ffn_conv_w[l], ffn_conv_b[l], w_dn_b, norm_final, tm=min(256, n_s),
        rows_per_seq=n_s, fix0=fix0, fix1=fix1)
    fconv_s = g_s.reshape(bd, t_new, d_ff)[:, t_new - (FFN_CONV_W - 1):, :]

    heads = lambda a, b_, t_: a.reshape(1, b_, t_, N_HEADS_A, HEAD_DIM)
    return (y_p.reshape(bp, s, d), y_s.reshape(bd, t_new, d),
            heads(k_p, bp, s), heads(v_p, bp, s), ki_p.reshape(1, bp, s, D_IDX),
            hlast_p.reshape(1, bp, d_lru), lconv_p[None], fconv_p[None],
            heads(k, bd, t_new), heads(v, bd, t_new), ki.reshape(1, bd, t_new, D_IDX),
            hlast_s.reshape(1, bd, d_lru), lconv_s[None], fconv_s[None])
```

```python
import functools
import math

import jax
import jax.numpy as jnp
import numpy as np
from jax import lax
from jax.experimental import pallas as pl
from jax.experimental.pallas import tpu as pltpu

F32 = jnp.float32
BF16 = jnp.bfloat16
I32 = jnp.int32

HEAD_DIM = 64
N_HEADS_A = 8
D_ATTN = HEAD_DIM * N_HEADS_A
N_IDX_HEADS = 8
D_IDX = 64
D_QI = N_IDX_HEADS * D_IDX
INDEX_SCALE = (N_IDX_HEADS * D_IDX) ** -0.5
TOPK_MAX = 256
N_HEADS_LRU = 8
LRU_CONV_W = 4
LRU_C = 8.0
N_BUCKETS = 32
MAX_DISTANCE = 128
FFN_CONV_W = 3
EPS = 1e-6
PAGE_SIZE = 128

FAR_DISTANCE = 113
NEG = -1e30
LOG2E = math.log2(math.e)
LOGIT_SCALE = HEAD_DIM ** -0.5 * LOG2E
NO_TIE_CUT = float(2 ** 30)

VMEM_LIMIT_BYTES = 56 * 1024 * 1024


def _params(sem):
    return pltpu.CompilerParams(dimension_semantics=sem, vmem_limit_bytes=VMEM_LIMIT_BYTES)


def _rmsnorm(x, g):
    ms = jnp.mean(x * x, axis=-1, keepdims=True)
    return (x * lax.rsqrt(ms + EPS)) * g


def _inproj_kernel(x_ref, g_ref, w_ref, qb_ref, k_ref, v_ref, kb_ref, vb_ref, qib_ref, ki_ref,
                   kib_ref, wi_ref, lx_ref, lg_ref, *, d_lru):
    xn = _rmsnorm(x_ref[...], g_ref[...])
    z = jnp.dot(xn.astype(BF16), w_ref[...], preferred_element_type=F32)
    o = 0
    q = z[:, o:o + D_ATTN]; o += D_ATTN
    k = z[:, o:o + D_ATTN]; o += D_ATTN
    v = z[:, o:o + D_ATTN]; o += D_ATTN
    qi = z[:, o:o + D_QI]; o += D_QI
    lx = z[:, o:o + d_lru]; o += d_lru
    lg = z[:, o:o + d_lru]; o += d_lru
    ki = z[:, o:o + D_IDX]; o += D_IDX
    wi = z[:, o:o + N_IDX_HEADS]
    qb_ref[...] = (q * LOGIT_SCALE).astype(BF16)
    k_ref[...] = k
    v_ref[...] = v
    kb_ref[...] = k.astype(BF16)
    vb_ref[...] = v.astype(BF16)
    qib_ref[...] = qi.astype(BF16)
    ki_ref[...] = ki
    kib_ref[...] = ki.astype(BF16)
    wi_ref[...] = wi
    lx_ref[...] = lx
    lg_ref[...] = lg


def _in_projection(x2d, norm_g, w_re, d_lru, tm):
    n, d = x2d.shape
    nc = w_re.shape[1]
    row = lambda w: pl.BlockSpec((tm, w), lambda i: (i, 0))
    widths = [(D_ATTN, BF16), (D_ATTN, F32), (D_ATTN, F32), (D_ATTN, BF16), (D_ATTN, BF16),
              (D_QI, BF16), (D_IDX, F32), (D_IDX, BF16), (N_IDX_HEADS, F32), (d_lru, F32), (d_lru, F32)]
    return pl.pallas_call(
        functools.partial(_inproj_kernel, d_lru=d_lru),
        grid=(n // tm,),
        in_specs=[row(d), pl.BlockSpec((1, d), lambda i: (0, 0)),
                  pl.BlockSpec((d, nc), lambda i: (0, 0))],
        out_specs=[row(w) for w, _ in widths],
        out_shape=[jax.ShapeDtypeStruct((n, w), dt) for w, dt in widths],
        compiler_params=_params(("parallel",)),
        name="in_projection",
    )(x2d, norm_g.reshape(1, d), w_re)


def _softplus(x):
    return jnp.maximum(x, 0.0) + jnp.log1p(jnp.exp(-jnp.abs(x)))


def _lru_kernel(lx_ref, lg_ref, buf_ref, h0_ref, cw_ref, cb_ref, wax_ref, bax_ref, lam_ref,
                y_ref, newbuf_ref, hlast_ref, xbuf, hcar, a_s, h_s, *, tc, d_lru):
    c = pl.program_id(1)

    @pl.when(c == 0)
    def _():
        xbuf[0:8, :] = buf_ref[0]
        hcar[...] = h0_ref[0]

    xbuf[8:8 + tc, :] = lx_ref[0]
    xc = xbuf[5:5 + tc, :] * cw_ref[0:1, :] + cb_ref[...]
    for j in range(1, LRU_CONV_W):
        xc = xc + xbuf[5 + j:5 + j + tc, :] * cw_ref[j:j + 1, :]
    gates = jnp.dot(xc.astype(BF16), wax_ref[...], preferred_element_type=F32) + bax_ref[...]
    r = jax.nn.sigmoid(gates[:, :d_lru])
    ig = jax.nn.sigmoid(gates[:, d_lru:])
    log_a = -LRU_C * r * _softplus(-lam_ref[...])
    a_s[...] = jnp.exp(log_a)
    neg_expm1 = -jnp.tanh(log_a) * (jnp.exp(2.0 * log_a) + 1.0)
    h_s[...] = jnp.sqrt(neg_expm1) * ig * xc

    def group(gi, h):
        r0 = pl.multiple_of(gi * 8, 8)
        a8 = a_s[pl.ds(r0, 8), :]
        b8 = h_s[pl.ds(r0, 8), :]
        rows = []
        for t in range(8):
            h = a8[t:t + 1, :] * h + b8[t:t + 1, :]
            rows.append(h)
        h_s[pl.ds(r0, 8), :] = jnp.concatenate(rows, axis=0)
        return h

    h = lax.fori_loop(0, tc // 8, group, hcar[...])
    hcar[...] = h
    y_ref[0] = (h_s[...] * jax.nn.gelu(lg_ref[0])).astype(y_ref.dtype)
    newbuf_ref[0] = xbuf[tc + 5:tc + 8, :]
    hlast_ref[0] = h
    xbuf[0:8, :] = xbuf[tc:tc + 8, :]


def _lru_branch(lx, lg, buf8, h0, conv_w, conv_b, wax, bax, lam, tc):
    b, t, c = lx.shape
    seq = lambda: pl.BlockSpec((1, tc, c), lambda i, j: (i, j, 0))
    per_b = lambda r: pl.BlockSpec((1, r, c), lambda i, j: (i, 0, 0))
    const = lambda shp: pl.BlockSpec(shp, lambda i, j: (0, 0))
    return pl.pallas_call(
        functools.partial(_lru_kernel, tc=tc, d_lru=c),
        grid=(b, t // tc),
        in_specs=[seq(), seq(), per_b(8), per_b(1), const((LRU_CONV_W, c)), const((1, c)),
                  const((c, 2 * c)), const((1, 2 * c)), const((1, c))],
        out_specs=[seq(), per_b(LRU_CONV_W - 1), per_b(1)],
        out_shape=[jax.ShapeDtypeStruct((b, t, c), BF16),
                   jax.ShapeDtypeStruct((b, LRU_CONV_W - 1, c), F32),
                   jax.ShapeDtypeStruct((b, 1, c), F32)],
        scratch_shapes=[pltpu.VMEM((tc + 8, c), F32), pltpu.VMEM((1, c), F32),
                        pltpu.VMEM((tc, c), F32), pltpu.VMEM((tc, c), F32)],
        compiler_params=_params(("parallel", "arbitrary")),
        name="rglru_branch",
    )(lx, lg, buf8, h0, conv_w, conv_b.reshape(1, c), wax, bax.reshape(1, 2 * c), lam.reshape(1, c))


def _t5_bucket(dist):
    n = jnp.maximum(dist, 0)
    max_exact = N_BUCKETS // 2
    nf = jnp.maximum(n, 1).astype(F32)
    large = max_exact + (jnp.log(nf / max_exact) / math.log(MAX_DISTANCE / max_exact)
                         * (N_BUCKETS - max_exact)).astype(I32)
    large = jnp.minimum(large, N_BUCKETS - 1)
    return jnp.where(n < max_exact, n, large)


def _bias_lookup(rb_ref, bucket, head):
    acc = jnp.zeros(bucket.shape, F32)
    for b in range(N_BUCKETS):
        acc = jnp.where(bucket == b, rb_ref[b * N_HEADS_A + head], acc)
    return acc


def _bias_tiles_kernel(rb_ref, o_ref, *, tk, tq, d0_min):
    kind = pl.program_id(0)
    head = pl.program_id(1)
    ss = lax.broadcasted_iota(I32, (tk, tq), 0)
    tt = lax.broadcasted_iota(I32, (tk, tq), 1)
    d = d0_min + kind * tk + tt - ss
    val = _bias_lookup(rb_ref, _t5_bucket(d), head)
    o_ref[0, 0] = jnp.where(d < 0, NEG, val * LOG2E)


def _prompt_bias_tiles(rel_bias, tk, tq):
    d0_min = tk - tq
    n_near = -(-(tk - 1 + FAR_DISTANCE - d0_min) // tk)
    tiles = pl.pallas_call(
        functools.partial(_bias_tiles_kernel, tk=tk, tq=tq, d0_min=d0_min),
        grid=(n_near, N_HEADS_A),
        in_specs=[pl.BlockSpec(memory_space=pltpu.SMEM)],
        out_specs=pl.BlockSpec((1, 1, tk, tq), lambda k, h: (k, h, 0, 0)),
        out_shape=jax.ShapeDtypeStruct((n_near, N_HEADS_A, tk, tq), F32),
        compiler_params=_params(("parallel", "parallel")),
        name="prompt_bias_tiles",
    )(rel_bias.reshape(-1))
    return tiles, n_near, d0_min


def _sample_bias_kernel(rb_ref, o_ref, *, t_new):
    kind = pl.program_id(0)
    rows = N_HEADS_A * t_new
    r = lax.broadcasted_iota(I32, (rows, PAGE_SIZE), 0)
    ss = lax.broadcasted_iota(I32, (rows, PAGE_SIZE), 1)
    tq = r % t_new
    head = r // t_new
    base = jnp.where(kind == 0, FAR_DISTANCE + PAGE_SIZE, jnp.where(kind == 1, PAGE_SIZE, 0))
    d = base + tq - ss
    bucket = _t5_bucket(d)
    acc = jnp.zeros((rows, PAGE_SIZE), F32)
    for h in range(N_HEADS_A):
        acc = jnp.where(head == h, _bias_lookup(rb_ref, bucket, h), acc)
    o_ref[0] = jnp.where(d < 0, NEG, acc * LOG2E)


def _sample_bias_tiles(rel_bias, t_new):
    rows = N_HEADS_A * t_new
    return pl.pallas_call(
        functools.partial(_sample_bias_kernel, t_new=t_new),
        grid=(3,),
        in_specs=[pl.BlockSpec(memory_space=pltpu.SMEM)],
        out_specs=pl.BlockSpec((1, rows, PAGE_SIZE), lambda k: (k, 0, 0)),
        out_shape=jax.ShapeDtypeStruct((3, rows, PAGE_SIZE), F32),
        compiler_params=_params(("parallel",)),
        name="sample_bias_tiles",
    )(rel_bias.reshape(-1))


def _tri_tables(n_q_blocks, tq, tk):
    ii, jj = [], []
    for i in range(n_q_blocks):
        for j in range((i + 1) * tq // tk):
            ii.append(i)
            jj.append(j)
    return np.asarray(ii, np.int32), np.asarray(jj, np.int32)


def _scores_kernel(ii_ref, jj_ref, ki_ref, qit_ref, wit_ref, o_ref, *, tk, tq):
    s = pl.program_id(1)
    i = ii_ref[s]
    j = jj_ref[s]
    ki = ki_ref[0]
    acc = jnp.zeros((tk, tq), F32)
    for h in range(N_IDX_HEADS):
        d = jnp.dot(ki, qit_ref[0, h * D_IDX:(h + 1) * D_IDX, :], preferred_element_type=F32)
        acc = acc + jnp.maximum(d, 0.0) * wit_ref[0, h:h + 1, :]
    sc = acc * INDEX_SCALE
    kpos = j * tk + lax.broadcasted_iota(I32, (tk, tq), 0)
    qpos = i * tq + lax.broadcasted_iota(I32, (tk, tq), 1)
    o_ref[0] = jnp.where(kpos <= qpos, sc, -jnp.inf)


def _prompt_scores(kib, qit, wit, tk, tq):
    b, s, _ = kib.shape
    ii, jj = _tri_tables(s // tq, tq, tk)
    grid_spec = pltpu.PrefetchScalarGridSpec(
        num_scalar_prefetch=2,
        grid=(b, len(ii)),
        in_specs=[pl.BlockSpec((1, tk, D_IDX), lambda bb, st, ii, jj: (bb, jj[st], 0)),
                  pl.BlockSpec((1, D_QI, tq), lambda bb, st, ii, jj: (bb, 0, ii[st])),
                  pl.BlockSpec((1, N_IDX_HEADS, tq), lambda bb, st, ii, jj: (bb, 0, ii[st]))],
        out_specs=pl.BlockSpec((1, tk, tq), lambda bb, st, ii, jj: (bb, jj[st], ii[st])),
    )
    return pl.pallas_call(
        functools.partial(_scores_kernel, tk=tk, tq=tq),
        grid_spec=grid_spec,
        out_shape=jax.ShapeDtypeStruct((b, s, s), F32),
        compiler_params=_params(("parallel", "arbitrary")),
        name="prompt_indexer_scores",
    )(jnp.asarray(ii), jnp.asarray(jj), kib, qit, wit)


def _to_key(x):
    bits = lax.bitcast_convert_type(x, I32)
    return jnp.where(bits < 0, -(bits & INT_MAX), bits)


def _from_key(key):
    bits = jnp.where(key < 0, (-key) | INT_MIN, key)
    return lax.bitcast_convert_type(bits, F32)


INT_MIN = -2 ** 31
INT_MAX = 2 ** 31 - 1
NEG_INF_KEY = -0x7F800000
COUNT_ROWS = 32


MIN_NORMAL_KEY = 0x00800000


def _skip_subnormal(m, down):
    if down:
        return jnp.where((m > 0) & (m < MIN_NORMAL_KEY), 0, jnp.where((m < 0) & (m > -MIN_NORMAL_KEY),
                                                                      -MIN_NORMAL_KEY, m))
    return jnp.where((m > 0) & (m < MIN_NORMAL_KEY), MIN_NORMAL_KEY,
                     jnp.where((m < 0) & (m > -MIN_NORMAL_KEY), 0, m))


def _bisect_kth(count_ge, lo, hi, cnt_lo, cnt_hi, k_sel, unit=1, max_in=2):
    coarse = unit > 1

    def active(lo, hi, cnt_lo, cnt_hi):
        open_ = lo + unit != hi
        if coarse:
            open_ = open_ & ~((lo == 0) & (hi == MIN_NORMAL_KEY)) & ~((lo == -MIN_NORMAL_KEY) & (hi == 0))
        return ((cnt_lo - cnt_hi) > max_in) & open_

    def n_active(*c):
        return jnp.max(active(*c).astype(I32))

    def body(c):
        lo, hi, cnt_lo, cnt_hi, _ = c
        act = active(lo, hi, cnt_lo, cnt_hi)
        mid = _to_key(0.5 * _from_key(lo) + 0.5 * _from_key(hi)) & -unit
        avg = ((lo >> 1) + (hi >> 1) + (lo & hi & 1)) & -unit
        if coarse:
            mid = _skip_subnormal(mid, down=False)
            avg = jnp.where(lo == 0, _skip_subnormal(avg, down=False), _skip_subnormal(avg, down=True))
        else:
            mid = jnp.where((lo == 0) & (hi > 1), 1, mid)
        mid = jnp.where((lo < 0) & (hi > 0), 0, mid)
        inside = (mid > lo) & (mid < hi)
        mid = jnp.where(inside, mid, avg)
        cnt = count_ge(mid)
        up = act & (cnt >= k_sel)
        dn = act & (cnt < k_sel)
        lo = jnp.where(up, mid, lo)
        cnt_lo = jnp.where(up, cnt, cnt_lo)
        hi = jnp.where(dn, mid, hi)
        cnt_hi = jnp.where(dn, cnt, cnt_hi)
        return lo, hi, cnt_lo, cnt_hi, n_active(lo, hi, cnt_lo, cnt_hi)

    out = lax.while_loop(lambda c: c[4] > 0, body,
                         (lo, hi, cnt_lo, cnt_hi, n_active(lo, hi, cnt_lo, cnt_hi)))
    return out[:4]


COARSE = 1 << 16


def _threshold_kernel(ii_ref, jj_ref, nj_ref, st_ref, o_ref, slab, slab16, gmax, *, tk, tq, k_sel, rows,
                      rows16, pos_bits):
    s = pl.program_id(1)
    j = jj_ref[s]
    nj = nj_ref[s]
    key = _to_key(st_ref[0])
    slab[j] = key
    slab16[j] = _from_key(key & -COARSE).astype(BF16)

    @pl.when(j == 0)
    def _():
        gmax[...] = key

    @pl.when(j > 0)
    def _():
        gmax[...] = jnp.maximum(gmax[...], key)

    def reduce_rows(read, n_blocks, init, step, final):
        def body(jb, acc):
            for rc in range(tk // rows):
                acc = step(acc, read(jb, rc), jb, rc)
            return acc
        return final(lax.fori_loop(0, n_blocks, body, jnp.full((rows, tq), init, I32)))

    sum_rows = lambda a: jnp.sum(a, axis=0, keepdims=True)
    read_slab = lambda jb, rc: slab[jb, rc * rows:(rc + 1) * rows, :]
    read_gmax = lambda jb, rc: gmax[rc * rows:(rc + 1) * rows, :]

    def count(read, n_blocks, pred):
        return reduce_rows(read, n_blocks, 0, lambda acc, blk, jb, rc: acc + jnp.where(pred(blk, jb, rc), 1, 0),
                           sum_rows)

    def count16(m):
        m16 = _from_key(m).astype(BF16)
        one = jnp.ones((rows16, tq), BF16)

        def body(jb, acc):
            for rc in range(tk // rows16):
                blk = slab16[jb, rc * rows16:(rc + 1) * rows16, :]
                acc = acc + jnp.where(blk >= m16, one, jnp.zeros_like(one))
            return acc

        acc = lax.fori_loop(0, nj, body, jnp.zeros((rows16, tq), BF16))
        return jnp.sum(acc.astype(F32), axis=0, keepdims=True).astype(I32)

    @pl.when(j == nj - 1)
    def _():
        g = gmax[...]
        top = jnp.max(g, axis=0, keepdims=True) + 1
        zero = jnp.zeros((1, tq), I32)
        if tk >= k_sel:
            g_lo = jnp.min(g, axis=0, keepdims=True)
            lo, _, _, _ = _bisect_kth(lambda m: count(read_gmax, 1, lambda blk, jb, rc: blk >= m),
                                      g_lo, top, jnp.full((1, tq), tk, I32), zero, k_sel, max_in=16)
        else:
            lo = jnp.full((1, tq), NEG_INF_KEY, I32)
        lo = _skip_subnormal(lo & -COARSE, down=True)
        top = _skip_subnormal((top + (COARSE - 1)) & -COARSE, down=False)
        lo, hi, cnt_lo, cnt_hi = _bisect_kth(count16, lo, top, count16(lo), zero, k_sel, unit=COARSE)
        count_ge = lambda m: count(read_slab, nj, lambda blk, jb, rc: blk >= m)
        lo, hi, cnt_lo, cnt_hi = _bisect_kth(count_ge, lo, hi, cnt_lo, cnt_hi, k_sel)
        mn = reduce_rows(read_slab, nj, INT_MAX,
                         lambda acc, blk, jb, rc: jnp.minimum(acc, jnp.where(blk >= lo, blk, INT_MAX)),
                         lambda a: jnp.min(a, axis=0, keepdims=True))
        mx = reduce_rows(read_slab, nj, INT_MIN,
                         lambda acc, blk, jb, rc: jnp.maximum(acc, jnp.where(blk < hi, blk, INT_MIN)),
                         lambda a: jnp.max(a, axis=0, keepdims=True))
        unit = lo + 1 == hi
        n_in = cnt_lo - cnt_hi
        rank = k_sel - cnt_hi
        kth = jnp.where(unit, lo, jnp.where(rank == 1, mx, mn))
        n_gt = jnp.where(unit | (rank == 1) | (mx == mn), cnt_hi, cnt_hi + 1)
        n_eq = jnp.where(unit, n_in, jnp.where((n_in == 2) & (mx == mn), 2, 1))
        need = k_sel - n_gt
        neg_inf_key = _to_key(jnp.full((1, tq), -jnp.inf, F32))
        tie_cut = (n_eq > need) & (kth != neg_inf_key)
        o_ref[0, 0:1, :] = _from_key(kth)
        o_ref[0, 1:2, :] = jnp.full((1, tq), NO_TIE_CUT, F32)
        o_ref[0, 2:8, :] = jnp.zeros((6, tq), F32)

        @pl.when(jnp.max(tie_cut.astype(I32)) > 0)
        def _():
            row = lax.broadcasted_iota(I32, (rows, tq), 0)

            def pos_step(it, cut):
                cand = cut | (jnp.int32(1) << (pos_bits - 1 - it))
                c = count(read_slab, nj,
                          lambda blk, jb, rc: (blk == kth) & (jb * tk + rc * rows + row < cand))
                return jnp.where(c < need, cand, cut)

            cut = lax.fori_loop(0, pos_bits, pos_step, jnp.zeros((1, tq), I32))
            o_ref[0, 1:2, :] = jnp.where(tie_cut, cut.astype(F32), NO_TIE_CUT)


def _topk_threshold(st, tk, tq, k_sel, causal):
    b, l, n = st.shape
    if causal:
        ii, jj = _tri_tables(n // tq, tq, tk)
    else:
        nb = l // tk
        ii = np.repeat(np.arange(n // tq, dtype=np.int32), nb)
        jj = np.tile(np.arange(nb, dtype=np.int32), n // tq)
    nj = np.asarray([(i + 1) * tq // tk if causal else l // tk for i in ii], np.int32)
    rows, rows16 = COUNT_ROWS, 2 * COUNT_ROWS
    n_blocks = int(nj.max())
    assert tk % rows16 == 0
    assert n_blocks * (tk // rows16) <= 256
    grid_spec = pltpu.PrefetchScalarGridSpec(
        num_scalar_prefetch=3,
        grid=(b, len(ii)),
        in_specs=[pl.BlockSpec((1, tk, tq), lambda bb, st_, ii, jj, nj: (bb, jj[st_], ii[st_]))],
        out_specs=pl.BlockSpec((1, 8, tq), lambda bb, st_, ii, jj, nj: (bb, 0, ii[st_])),
        scratch_shapes=[pltpu.VMEM((n_blocks, tk, tq), I32), pltpu.VMEM((n_blocks, tk, tq), BF16),
                        pltpu.VMEM((tk, tq), I32)],
    )
    slab_bytes = n_blocks * tk * tq * 6
    return pl.pallas_call(
        functools.partial(_threshold_kernel, tk=tk, tq=tq, k_sel=k_sel, rows=rows, rows16=rows16,
                          pos_bits=max(1, (l - 1).bit_length())),
        grid_spec=grid_spec,
        out_shape=jax.ShapeDtypeStruct((b, 8, n), F32),
        compiler_params=pltpu.CompilerParams(
            dimension_semantics=("parallel", "arbitrary"),
            vmem_limit_bytes=max(VMEM_LIMIT_BYTES, slab_bytes + 10 * 1024 * 1024)),
        name="topk_threshold",
    )(jnp.asarray(ii), jnp.asarray(jj), jnp.asarray(nj), st)


def _selected(score, key_pos, thr, cut):
    return (score > thr) | ((score == thr) & (key_pos <= cut))


ACC_ROWS = HEAD_DIM + 16


def _prompt_attn_kernel(ii_ref, jj_ref, nj_ref, cut_ref, kb_ref, qt_ref, vt_ref, st_ref, sel_ref, bias_ref,
                        far_ref, o_ref, m_s, acc_s, mb_s, lg_s, *, tk, tq, n_near, d0_min, n_q_blocks):
    b = pl.program_id(0)
    s = pl.program_id(1)
    i = ii_ref[s]
    j = jj_ref[s]
    nj = nj_ref[s]

    @pl.when(j == 0)
    def _():
        m_s[...] = jnp.full(m_s.shape, NEG, F32)
        acc_s[...] = jnp.zeros(acc_s.shape, F32)

    thr = sel_ref[0, 0:1, :]
    has_cut = cut_ref[b * n_q_blocks + i] > 0

    @pl.when(jnp.logical_not(has_cut))
    def _():
        mb_s[...] = jnp.where(st_ref[0] >= thr, 0.0, NEG)

    @pl.when(has_cut)
    def _():
        key_pos = (j * tk + lax.broadcasted_iota(I32, (tk, tq), 0)).astype(F32)
        mb_s[...] = jnp.where(_selected(st_ref[0], key_pos, thr, sel_ref[0, 1:2, :]), 0.0, NEG)

    kind = (i * tq - j * tk - d0_min) // tk
    ones = jnp.ones((ACC_ROWS - HEAD_DIM, tk), BF16)

    def qk(h):
        hs = slice(h * HEAD_DIM, (h + 1) * HEAD_DIM)
        lg_s[h % 2] = jnp.dot(kb_ref[0, :, hs], qt_ref[0, hs, :], preferred_element_type=F32)

    def heads(far):
        qk(0)
        for h in range(N_HEADS_A):
            if h + 1 < N_HEADS_A:
                qk(h + 1)
            lg = lg_s[h % 2] + mb_s[...]
            if far:
                shift = far_ref[h] * LOG2E
            else:
                lg = lg + bias_ref[kind, h]
                shift = 0.0
            m_old = m_s[h:h + 1, :]
            m_new = jnp.maximum(m_old, jnp.max(lg, axis=0, keepdims=True) + shift)
            alpha = jnp.exp2(m_old - m_new)
            p = jnp.exp2(lg - (m_new - shift)).astype(BF16)
            v_aug = jnp.concatenate([vt_ref[0, h * HEAD_DIM:(h + 1) * HEAD_DIM, :], ones], axis=0)
            rs = slice(h * ACC_ROWS, (h + 1) * ACC_ROWS)
            acc_s[rs, :] = alpha * acc_s[rs, :] + jnp.dot(v_aug, p, preferred_element_type=F32)
            m_s[h:h + 1, :] = m_new

    pl.when(kind >= n_near)(lambda: heads(True))
    pl.when(kind < n_near)(lambda: heads(False))

    @pl.when(j == nj - 1)
    def _():
        for h in range(N_HEADS_A):
            r0 = h * ACC_ROWS
            o_ref[0, h * HEAD_DIM:(h + 1) * HEAD_DIM, :] = (
                acc_s[r0:r0 + HEAD_DIM, :] / acc_s[r0 + HEAD_DIM:r0 + HEAD_DIM + 1, :]).astype(o_ref.dtype)


def _prompt_attention(kb, qt, vt, st, sel, has_cut, bias_tiles, far_bias, n_near, d0_min, tk, tq):
    b, s, d = kb.shape
    ii, jj = _tri_tables(s // tq, tq, tk)
    nj = np.asarray([(i + 1) * tq // tk for i in ii], np.int32)
    grid_spec = pltpu.PrefetchScalarGridSpec(
        num_scalar_prefetch=4,
        grid=(b, len(ii)),
        in_specs=[pl.BlockSpec((1, tk, d), lambda bb, t, ii, jj, nj, hc: (bb, jj[t], 0)),
                  pl.BlockSpec((1, d, tq), lambda bb, t, ii, jj, nj, hc: (bb, 0, ii[t])),
                  pl.BlockSpec((1, d, tk), lambda bb, t, ii, jj, nj, hc: (bb, 0, jj[t])),
                  pl.BlockSpec((1, tk, tq), lambda bb, t, ii, jj, nj, hc: (bb, jj[t], ii[t])),
                  pl.BlockSpec((1, 8, tq), lambda bb, t, ii, jj, nj, hc: (bb, 0, ii[t])),
                  pl.BlockSpec((n_near, N_HEADS_A, tk, tq), lambda bb, t, ii, jj, nj, hc: (0, 0, 0, 0),
                               pipeline_mode=pl.Buffered(1)),
                  pl.BlockSpec(memory_space=pltpu.SMEM)],
        out_specs=pl.BlockSpec((1, d, tq), lambda bb, t, ii, jj, nj, hc: (bb, 0, ii[t])),
        scratch_shapes=[pltpu.VMEM((N_HEADS_A, tq), F32), pltpu.VMEM((N_HEADS_A * ACC_ROWS, tq), F32),
                        pltpu.VMEM((tk, tq), F32), pltpu.VMEM((2, tk, tq), F32)],
    )
    return pl.pallas_call(
        functools.partial(_prompt_attn_kernel, tk=tk, tq=tq, n_near=n_near, d0_min=d0_min,
                          n_q_blocks=s // tq),
        grid_spec=grid_spec,
        out_shape=jax.ShapeDtypeStruct((b, d, s), BF16),
        compiler_params=_params(("parallel", "arbitrary")),
        name="prompt_sparse_attention",
    )(jnp.asarray(ii), jnp.asarray(jj), jnp.asarray(nj), has_cut, kb, qt, vt, st, sel, bias_tiles, far_bias)


PAGES_PER_STEP = 16


def _page_spec(block, u):
    zeros = (0,) * (len(block) - 1)
    return pl.BlockSpec(block, lambda sq, g, pt: (pt[sq, g * PAGES_PER_STEP + u],) + zeros)


def _head_sum(x, t_new):
    acc = x[0:t_new, :]
    for h in range(1, N_IDX_HEADS):
        acc = acc + x[h * t_new:(h + 1) * t_new, :]
    return acc


def _sample_scores_kernel(pt_ref, *refs, t_new):
    pages = refs[:PAGES_PER_STEP]
    qi_ref, w_ref, kinew_ref, o_ref, onew_ref = refs[PAGES_PER_STEP:]
    qi = qi_ref[0]
    w = w_ref[0]

    def score(d):
        return _head_sum(jnp.maximum(d, 0.0) * w, t_new) * INDEX_SCALE

    for u in range(PAGES_PER_STEP):
        o_ref[0, :, u * PAGE_SIZE:(u + 1) * PAGE_SIZE] = score(
            jnp.dot(qi, pages[u][0].astype(BF16), preferred_element_type=F32))

    sc = score(lax.dot_general(qi, kinew_ref[0], (((1,), (1,)), ((), ())), preferred_element_type=F32))
    tq = lax.broadcasted_iota(I32, (t_new, PAGE_SIZE), 0)
    ss = lax.broadcasted_iota(I32, (t_new, PAGE_SIZE), 1)
    onew_ref[0] = jnp.where(ss <= tq, sc, -jnp.inf)


def _sample_scores(page_table, cache_idx, qi_rows, w_rows, ki_new_pad, t_new):
    bd, n_pages = page_table.shape
    rows = N_IDX_HEADS * t_new
    per_seq = lambda shp: pl.BlockSpec((1,) + shp, lambda sq, g, pt: (sq, 0, 0))
    grid_spec = pltpu.PrefetchScalarGridSpec(
        num_scalar_prefetch=1,
        grid=(bd, n_pages // PAGES_PER_STEP),
        in_specs=[_page_spec((1, D_IDX, PAGE_SIZE), u) for u in range(PAGES_PER_STEP)]
        + [per_seq((rows, D_IDX)), per_seq((rows, 1)), per_seq((PAGE_SIZE, D_IDX))],
        out_specs=[pl.BlockSpec((1, t_new, PAGES_PER_STEP * PAGE_SIZE), lambda sq, g, pt: (sq, 0, g)),
                   per_seq((t_new, PAGE_SIZE))],
    )
    return pl.pallas_call(
        functools.partial(_sample_scores_kernel, t_new=t_new),
        grid_spec=grid_spec,
        out_shape=[jax.ShapeDtypeStruct((bd, t_new, n_pages * PAGE_SIZE), F32),
                   jax.ShapeDtypeStruct((bd, t_new, PAGE_SIZE), F32)],
        compiler_params=_params(("parallel", "arbitrary")),
        name="sample_indexer_scores",
    )(page_table, *([cache_idx] * PAGES_PER_STEP), qi_rows, w_rows, ki_new_pad)


def _sample_attn_kernel(pt_ref, *refs, t_new, n_groups, far_pages):
    kp = refs[:PAGES_PER_STEP]
    vp = refs[PAGES_PER_STEP:2 * PAGES_PER_STEP]
    (q_ref, knew_ref, vnew_ref, sc_ref, scnew_ref, sel_ref, bias_ref, o_ref,
     qb_s, m_s, l_s, acc_s) = refs[2 * PAGES_PER_STEP:]
    g = pl.program_id(1)
    rows = N_HEADS_A * t_new
    d = N_HEADS_A * HEAD_DIM

    @pl.when(g == 0)
    def _():
        q_all = jnp.concatenate([q_ref[0].astype(F32)] * N_HEADS_A, axis=0)
        r = lax.broadcasted_iota(I32, (rows, d), 0)
        c = lax.broadcasted_iota(I32, (rows, d), 1)
        qb_s[...] = jnp.where(r // t_new == c // HEAD_DIM, q_all, 0.0).astype(BF16)
        m_s[...] = jnp.full(m_s.shape, NEG, F32)
        l_s[...] = jnp.zeros(l_s.shape, F32)
        acc_s[...] = jnp.zeros(acc_s.shape, F32)

    thr = sel_ref[0, :, 0:1]
    cut = sel_ref[0, :, 1:2]
    qb = qb_s[...]

    contract_last = (((1,), (1,)), ((), ()))

    def logits(qk, score, key_pos0, bias):
        pos = (key_pos0 + lax.broadcasted_iota(I32, (t_new, PAGE_SIZE), 1)).astype(F32)
        mb = jnp.where(_selected(score, pos, thr, cut), 0.0, NEG)
        return qk + bias + jnp.concatenate([mb] * N_HEADS_A, axis=0)

    def update(lg, values_bf, channel_major):
        m_old = m_s[...]
        m_new = jnp.maximum(m_old, jnp.max(lg, axis=1, keepdims=True))
        alpha = jnp.exp2(m_old - m_new)
        p = jnp.exp2(lg - m_new)
        l_s[...] = alpha * l_s[...] + jnp.sum(p, axis=1, keepdims=True)
        pv = jnp.zeros((rows, d), F32)
        for u, vb in enumerate(values_bf):
            pu = p[:, u * PAGE_SIZE:(u + 1) * PAGE_SIZE].astype(BF16)
            if channel_major:
                pv = pv + lax.dot_general(pu, vb, contract_last, preferred_element_type=F32)
            else:
                pv = pv + jnp.dot(pu, vb, preferred_element_type=F32)
        acc_s[...] = alpha * acc_s[...] + pv
        m_s[...] = m_new

    lgs = []
    for u in range(PAGES_PER_STEP):
        page = g * PAGES_PER_STEP + u
        bias = bias_ref[jnp.where(page < far_pages, 0, 1)]
        qk = jnp.dot(qb, kp[u][0].astype(BF16), preferred_element_type=F32)
        lgs.append(logits(qk, sc_ref[0, :, u * PAGE_SIZE:(u + 1) * PAGE_SIZE], page * PAGE_SIZE, bias))
    update(jnp.concatenate(lgs, axis=1), [vp[u][0].astype(BF16) for u in range(PAGES_PER_STEP)], True)

    @pl.when(g == n_groups - 1)
    def _():
        past = n_groups * PAGES_PER_STEP * PAGE_SIZE
        qk = lax.dot_general(qb, knew_ref[0], contract_last, preferred_element_type=F32)
        update(logits(qk, scnew_ref[0], past, bias_ref[2]), [vnew_ref[0]], False)
        out = []
        for h in range(N_HEADS_A):
            rs = slice(h * t_new, (h + 1) * t_new)
            out.append(acc_s[rs, h * HEAD_DIM:(h + 1) * HEAD_DIM] / l_s[rs, :])
        o_ref[0] = jnp.concatenate(out, axis=1).astype(o_ref.dtype)


def _sample_attention(page_table, cache_k, cache_v, q_s, k_new_pad, v_new_pad, sc_past, sc_new,
                      sel_rows, bias_tiles, t_new):
    bd, n_pages = page_table.shape
    d = cache_k.shape[1]
    rows = N_HEADS_A * t_new
    n_groups = n_pages // PAGES_PER_STEP
    far_pages = n_pages - 1
    assert PAGE_SIZE + 1 >= FAR_DISTANCE
    per_seq = lambda shp: pl.BlockSpec((1,) + shp, lambda sq, g, pt: (sq, 0, 0))
    grid_spec = pltpu.PrefetchScalarGridSpec(
        num_scalar_prefetch=1,
        grid=(bd, n_groups),
        in_specs=[_page_spec((1, d, PAGE_SIZE), u) for u in range(PAGES_PER_STEP)] * 2
        + [per_seq((t_new, d)), per_seq((PAGE_SIZE, d)), per_seq((PAGE_SIZE, d)),
           pl.BlockSpec((1, t_new, PAGES_PER_STEP * PAGE_SIZE), lambda sq, g, pt: (sq, 0, g)),
           per_seq((t_new, PAGE_SIZE)), per_seq((t_new, 128)),
           pl.BlockSpec((3, rows, PAGE_SIZE), lambda sq, g, pt: (0, 0, 0))],
        out_specs=per_seq((t_new, d)),
        scratch_shapes=[pltpu.VMEM((rows, d), BF16), pltpu.VMEM((rows, 1), F32),
                        pltpu.VMEM((rows, 1), F32), pltpu.VMEM((rows, d), F32)],
    )
    return pl.pallas_call(
        functools.partial(_sample_attn_kernel, t_new=t_new, n_groups=n_groups, far_pages=far_pages),
        grid_spec=grid_spec,
        out_shape=jax.ShapeDtypeStruct((bd, t_new, d), BF16),
        compiler_params=_params(("parallel", "arbitrary")),
        name="sample_sparse_attention",
    )(page_table, *([cache_k] * PAGES_PER_STEP), *([cache_v] * PAGES_PER_STEP), q_s, k_new_pad,
      v_new_pad, sc_past, sc_new, sel_rows, bias_tiles)


def _ffn_kernel(*refs, tm, d_ff, tiles_per_seq, fixup):
    if fixup:
        (x_ref, a_ref, r_ref, wo_ref, gf_ref, wup_ref, cw_ref, cb_ref, wdn_ref, gl_ref, x0_ref, x1_ref,
         y_ref, g_ref, gbuf) = refs
    else:
        (x_ref, a_ref, r_ref, wo_ref, gf_ref, wup_ref, cw_ref, cb_ref, wdn_ref, gl_ref,
         y_ref, g_ref, gbuf) = refs
    i = pl.program_id(0)
    d_attn = a_ref.shape[1]
    x1 = (x_ref[...] + jnp.dot(a_ref[...], wo_ref[0:d_attn, :], preferred_element_type=F32)
          + jnp.dot(r_ref[...], wo_ref[d_attn:, :], preferred_element_type=F32))
    xn = _rmsnorm(x1, gf_ref[...]).astype(BF16)
    up = jnp.dot(xn, wup_ref[...], preferred_element_type=F32)
    g = up[:, :d_ff]
    u = up[:, d_ff:]

    @pl.when(i % tiles_per_seq == 0)
    def _():
        gbuf[0:8, :] = jnp.zeros((8, d_ff), F32)

    gbuf[8:8 + tm, :] = g
    prev2 = gbuf[6:6 + tm, :]
    prev1 = gbuf[7:7 + tm, :]
    if fixup:
        t = lax.broadcasted_iota(I32, (tm, d_ff), 0) % 8
        prev2 = jnp.where(t < 2, x0_ref[...], prev2)
        prev1 = jnp.where(t < 1, x1_ref[...], prev1)
    gc = prev2 * cw_ref[0:1, :] + cb_ref[...]
    gc = gc + prev1 * cw_ref[1:2, :]
    gc = gc + g * cw_ref[2:3, :]
    hid = (jax.nn.gelu(gc) * u).astype(BF16)
    x2 = x1 + jnp.dot(hid, wdn_ref[...], preferred_element_type=F32)
    y_ref[...] = _rmsnorm(x2, gl_ref[...])
    if fixup:
        g_ref[...] = g
    else:
        g_ref[0] = g[tm - (FFN_CONV_W - 1):, :]
    gbuf[0:8, :] = gbuf[tm:tm + 8, :]


def _merge_and_ffn(x2d, attn, lru, w_out, norm_ffn, w_up, conv_w, conv_b, w_down, norm_final, tm,
                   rows_per_seq, fix0=None, fix1=None):
    n, d = x2d.shape
    d_ff = w_down.shape[0]
    fixup = fix0 is not None
    tiles_per_seq = max(rows_per_seq // tm, 1)
    row = lambda w: pl.BlockSpec((tm, w), lambda i: (i, 0))
    const = lambda a: pl.BlockSpec(a.shape, lambda i: (0,) * a.ndim, pipeline_mode=pl.Buffered(1))
    cb2 = conv_b.reshape(1, d_ff)
    gf2 = norm_ffn.reshape(1, d)
    gl2 = norm_final.reshape(1, d)
    ins = [x2d, attn, lru, w_out, gf2, w_up, conv_w, cb2, w_down, gl2]
    in_specs = [row(d), row(attn.shape[1]), row(lru.shape[1]), const(w_out), const(gf2), const(w_up),
                const(conv_w), const(cb2), const(w_down), const(gl2)]
    if fixup:
        ins += [fix0, fix1]
        in_specs += [row(d_ff), row(d_ff)]
        g_spec = row(d_ff)
        g_shape = jax.ShapeDtypeStruct((n, d_ff), F32)
    else:
        g_spec = pl.BlockSpec((1, FFN_CONV_W - 1, d_ff), lambda i: (i // tiles_per_seq, 0, 0))
        g_shape = jax.ShapeDtypeStruct((n // rows_per_seq, FFN_CONV_W - 1, d_ff), F32)
    return pl.pallas_call(
        functools.partial(_ffn_kernel, tm=tm, d_ff=d_ff, tiles_per_seq=tiles_per_seq, fixup=fixup),
        grid=(n // tm,),
        in_specs=in_specs,
        out_specs=[row(d), g_spec],
        out_shape=[jax.ShapeDtypeStruct((n, d), F32), g_shape],
        scratch_shapes=[pltpu.VMEM((tm + 8, d_ff), F32)],
        compiler_params=_params(("arbitrary",)),
        name="merge_and_conv_ffn",
    )(*ins)


PROMPT_TILE = 512


def _key_tile(n_keys, unit=64, largest=1024):
    return max(t for t in range(unit, largest + 1, unit) if n_keys % t == 0)


def _block_diag(w):
    h, a, b = w.shape
    eye = jnp.eye(h, dtype=w.dtype)
    return (w[:, :, None, :] * eye[:, None, :, None]).reshape(h * a, h * b)


def kernel(x_prompt, x_sample, cache_k, cache_v, cache_idx_k, state_lru_h, state_lru_conv, state_ffn_conv, page_table, w_in, w_out, norm_mix, norm_ffn, norm_final, rel_bias, lru_conv_w, lru_conv_b, lru_w_a, lru_b_a, lru_w_x, lru_b_x, lru_lambda, ffn_w_up, ffn_conv_w, ffn_conv_b, ffn_w_down):
    bp, s, d = x_prompt.shape
    bd, t_new, _ = x_sample.shape
    depth = w_in.shape[0]
    assert depth == 1 and t_new == 8
    d_lru = lru_lambda.shape[1]
    d_ff = ffn_w_down.shape[1]
    n_pages = page_table.shape[1]
    past = n_pages * PAGE_SIZE
    l = 0

    c_qi_end = 3 * D_ATTN + D_QI
    c_wi_end = c_qi_end + D_IDX + N_IDX_HEADS
    pad_cols = (-(D_IDX + N_IDX_HEADS)) % 128
    w_re = jnp.concatenate([w_in[l][:, :c_qi_end], w_in[l][:, c_wi_end:], w_in[l][:, c_qi_end:c_wi_end],
                            jnp.zeros((d, pad_cols), w_in.dtype)], axis=1).astype(BF16)
    wax = jnp.concatenate([_block_diag(lru_w_a[l]), _block_diag(lru_w_x[l])], axis=1).astype(BF16)
    bax = jnp.concatenate([lru_b_a[l], lru_b_x[l]])
    w_out_b = w_out[l].astype(BF16)
    w_up_b = ffn_w_up[l].astype(BF16)
    w_dn_b = ffn_w_down[l].astype(BF16)

    tk = tq = PROMPT_TILE
    (qb, k, v, kb, vb, qib, ki, kib, wi, lx, lg) = _in_projection(
        x_prompt.reshape(bp * s, d), norm_mix[l], w_re, d_lru, tm=512)
    r3 = lambda a: a.reshape(bp, s, a.shape[-1])
    tr = lambda a: jnp.swapaxes(r3(a), 1, 2)
    st = _prompt_scores(r3(kib), tr(qib), tr(wi), tk, tq)
    sel = _topk_threshold(st, tk, tq, min(TOPK_MAX, s // 4), causal=True)
    has_cut = jnp.any((sel[:, 1, :] < NO_TIE_CUT).reshape(bp * (s // tq), tq), axis=1).astype(I32)
    bias_tiles, n_near, d0_min = _prompt_bias_tiles(rel_bias, tk, tq)
    attn_t = _prompt_attention(r3(kb), tr(qb), tr(vb), st, sel, has_cut, bias_tiles,
                               rel_bias[N_BUCKETS - 1], n_near, d0_min, tk, tq)
    attn_p = jnp.swapaxes(attn_t, 1, 2).reshape(bp * s, D_ATTN)
    lru_p, lconv_p, hlast_p = _lru_branch(
        r3(lx), r3(lg), jnp.zeros((bp, 8, d_lru), F32), jnp.zeros((bp, 1, d_lru), F32),
        lru_conv_w[l], lru_conv_b[l], wax, bax, lru_lambda[l], tc=512)
    y_p, fconv_p = _merge_and_ffn(
        x_prompt.reshape(bp * s, d), attn_p, lru_p.reshape(bp * s, d_lru), w_out_b, norm_ffn[l], w_up_b,
        ffn_conv_w[l], ffn_conv_b[l], w_dn_b, norm_final, tm=256, rows_per_seq=s)
    k_p, v_p, ki_p = k, v, ki

    n_s = bd * t_new
    (qb, k, v, kb, vb, qib, ki, kib, wi, lx, lg) = _in_projection(
        x_sample.reshape(n_s, d), norm_mix[l], w_re, d_lru, tm=min(512, n_s))
    s3 = lambda a: a.reshape(bd, t_new, a.shape[-1])
    qi_rows = s3(qib).reshape(bd, t_new, N_IDX_HEADS, D_IDX).transpose(0, 2, 1, 3).reshape(
        bd, N_IDX_HEADS * t_new, D_IDX)
    w_rows = s3(wi).transpose(0, 2, 1).reshape(bd, N_IDX_HEADS * t_new, 1)
    pad_new = lambda a: jnp.pad(s3(a), ((0, 0), (0, PAGE_SIZE - t_new), (0, 0)))
    idx_pages = jnp.swapaxes(cache_idx_k[l], 1, 2)
    k_pages = cache_k[l].transpose(0, 2, 3, 1).reshape(-1, D_ATTN, PAGE_SIZE)
    v_pages = cache_v[l].transpose(0, 2, 3, 1).reshape(-1, D_ATTN, PAGE_SIZE)
    sc_past, sc_new = _sample_scores(page_table, idx_pages, qi_rows, w_rows, pad_new(kib), t_new)
    l_keys = past + PAGE_SIZE
    sc_all = jnp.concatenate([sc_past, sc_new], axis=2).reshape(n_s, l_keys)
    sel_s = _topk_threshold(sc_all.T[None], _key_tile(l_keys), min(256, n_s),
                            min(TOPK_MAX, (past + t_new) // 4), causal=False)
    sel_rows = jnp.pad(sel_s[0].T.reshape(bd, t_new, 8), ((0, 0), (0, 0), (0, 120)))
    attn_s = _sample_attention(
        page_table, k_pages, v_pages, s3(qb), pad_new(kb), pad_new(vb), sc_past, sc_new, sel_rows,
        _sample_bias_tiles(rel_bias, t_new), t_new)
    buf8 = jnp.pad(state_lru_conv[l], ((0, 0), (8 - (LRU_CONV_W - 1), 0), (0, 0)))
    lru_s, lconv_s, hlast_s = _lru_branch(
        s3(lx), s3(lg), buf8, state_lru_h[l][:, None, :], lru_conv_w[l], lru_conv_b[l], wax, bax,
        lru_lambda[l], tc=t_new)
    fbuf = state_ffn_conv[l]
    fix0 = jnp.pad(fbuf, ((0, 0), (0, t_new - 2), (0, 0))).reshape(n_s, d_ff)
    fix1 = jnp.pad(fbuf[:, 1:2], ((0, 0), (0, t_new - 1), (0, 0))).reshape(n_s, d_ff)
    y_s, g_s = _merge_and_ffn(
        x_sample.reshape(n_s, d), attn_s.reshape(n_s, D_ATTN), lru_s.reshape(n_s, d_lru), w_out_b,
        norm_ffn[l], w_up_b, ffn_conv_w[l], ffn_conv_b[l], w_dn_b, norm_final, tm=min(256, n_s),
        rows_per_seq=n_s, fix0=fix0, fix1=fix1)
    fconv_s = g_s.reshape(bd, t_new, d_ff)[:, t_new - (FFN_CONV_W - 1):, :]

    heads = lambda a, b_, t_: a.reshape(1, b_, t_, N_HEADS_A, HEAD_DIM)
    return (y_p.reshape(bp, s, d), y_s.reshape(bd, t_new, d),
            heads(k_p, bp, s), heads(v_p, bp, s), ki_p.reshape(1, bp, s, D_IDX),
            hlast_p.reshape(1, bp, d_lru), lconv_p[None], fconv_p[None],
            heads(k, bd, t_new), heads(v, bd, t_new), ki.reshape(1, bd, t_new, D_IDX),
            hlast_s.reshape(1, bd, d_lru), lconv_s[None], fconv_s[None])
```

```python
import functools
import math

import jax
import jax.numpy as jnp
import numpy as np
from jax import lax
from jax.experimental import pallas as pl
from jax.experimental.pallas import tpu as pltpu

F32 = jnp.float32
BF16 = jnp.bfloat16
I32 = jnp.int32

HEAD_DIM = 64
N_HEADS_A = 8
D_ATTN = HEAD_DIM * N_HEADS_A
N_IDX_HEADS = 8
D_IDX = 64
D_QI = N_IDX_HEADS * D_IDX
INDEX_SCALE = (N_IDX_HEADS * D_IDX) ** -0.5
TOPK_MAX = 256
N_HEADS_LRU = 8
LRU_CONV_W = 4
LRU_C = 8.0
N_BUCKETS = 32
MAX_DISTANCE = 128
FFN_CONV_W = 3
EPS = 1e-6
PAGE_SIZE = 128

FAR_DISTANCE = 113
NEG = -1e30
LOG2E = math.log2(math.e)
LOGIT_SCALE = HEAD_DIM ** -0.5 * LOG2E
NO_TIE_CUT = float(2 ** 30)

VMEM_LIMIT_BYTES = 56 * 1024 * 1024


def _params(sem):
    return pltpu.CompilerParams(dimension_semantics=sem, vmem_limit_bytes=VMEM_LIMIT_BYTES)


def _rmsnorm(x, g):
    ms = jnp.mean(x * x, axis=-1, keepdims=True)
    return (x * lax.rsqrt(ms + EPS)) * g


def _inproj_kernel(x_ref, g_ref, w_ref, qb_ref, k_ref, v_ref, kb_ref, vb_ref, qib_ref, ki_ref,
                   kib_ref, wi_ref, lx_ref, lg_ref, *, d_lru):
    xn = _rmsnorm(x_ref[...], g_ref[...])
    z = jnp.dot(xn.astype(BF16), w_ref[...], preferred_element_type=F32)
    o = 0
    q = z[:, o:o + D_ATTN]; o += D_ATTN
    k = z[:, o:o + D_ATTN]; o += D_ATTN
    v = z[:, o:o + D_ATTN]; o += D_ATTN
    qi = z[:, o:o + D_QI]; o += D_QI
    lx = z[:, o:o + d_lru]; o += d_lru
    lg = z[:, o:o + d_lru]; o += d_lru
    ki = z[:, o:o + D_IDX]; o += D_IDX
    wi = z[:, o:o + N_IDX_HEADS]
    qb_ref[...] = (q * LOGIT_SCALE).astype(BF16)
    k_ref[...] = k
    v_ref[...] = v
    kb_ref[...] = k.astype(BF16)
    vb_ref[...] = v.astype(BF16)
    qib_ref[...] = qi.astype(BF16)
    ki_ref[...] = ki
    kib_ref[...] = ki.astype(BF16)
    wi_ref[...] = wi
    lx_ref[...] = lx
    lg_ref[...] = lg


def _in_projection(x2d, norm_g, w_re, d_lru, tm):
    n, d = x2d.shape
    nc = w_re.shape[1]
    row = lambda w: pl.BlockSpec((tm, w), lambda i: (i, 0))
    widths = [(D_ATTN, BF16), (D_ATTN, F32), (D_ATTN, F32), (D_ATTN, BF16), (D_ATTN, BF16),
              (D_QI, BF16), (D_IDX, F32), (D_IDX, BF16), (N_IDX_HEADS, F32), (d_lru, F32), (d_lru, F32)]
    return pl.pallas_call(
        functools.partial(_inproj_kernel, d_lru=d_lru),
        grid=(n // tm,),
        in_specs=[row(d), pl.BlockSpec((1, d), lambda i: (0, 0)),
                  pl.BlockSpec((d, nc), lambda i: (0, 0))],
        out_specs=[row(w) for w, _ in widths],
        out_shape=[jax.ShapeDtypeStruct((n, w), dt) for w, dt in widths],
        compiler_params=_params(("parallel",)),
        name="in_projection",
    )(x2d, norm_g.reshape(1, d), w_re)


def _softplus(x):
    return jnp.maximum(x, 0.0) + jnp.log1p(jnp.exp(-jnp.abs(x)))


def _lru_kernel(lx_ref, lg_ref, buf_ref, h0_ref, cw_ref, cb_ref, wax_ref, bax_ref, lam_ref,
                y_ref, newbuf_ref, hlast_ref, xbuf, hcar, a_s, h_s, *, tc, d_lru):
    c = pl.program_id(1)

    @pl.when(c == 0)
    def _():
        xbuf[0:8, :] = buf_ref[0]
        hcar[...] = h0_ref[0]

    xbuf[8:8 + tc, :] = lx_ref[0]
    xc = xbuf[5:5 + tc, :] * cw_ref[0:1, :] + cb_ref[...]
    for j in range(1, LRU_CONV_W):
        xc = xc + xbuf[5 + j:5 + j + tc, :] * cw_ref[j:j + 1, :]
    gates = jnp.dot(xc.astype(BF16), wax_ref[...], preferred_element_type=F32) + bax_ref[...]
    r = jax.nn.sigmoid(gates[:, :d_lru])
    ig = jax.nn.sigmoid(gates[:, d_lru:])
    log_a = -LRU_C * r * _softplus(-lam_ref[...])
    a_s[...] = jnp.exp(log_a)
    neg_expm1 = -jnp.tanh(log_a) * (jnp.exp(2.0 * log_a) + 1.0)
    h_s[...] = jnp.sqrt(neg_expm1) * ig * xc

    def group(gi, h):
        r0 = pl.multiple_of(gi * 8, 8)
        a8 = a_s[pl.ds(r0, 8), :]
        b8 = h_s[pl.ds(r0, 8), :]
        rows = []
        for t in range(8):
            h = a8[t:t + 1, :] * h + b8[t:t + 1, :]
            rows.append(h)
        h_s[pl.ds(r0, 8), :] = jnp.concatenate(rows, axis=0)
        return h

    h = lax.fori_loop(0, tc // 8, group, hcar[...])
    hcar[...] = h
    y_ref[0] = (h_s[...] * jax.nn.gelu(lg_ref[0])).astype(y_ref.dtype)
    newbuf_ref[0] = xbuf[tc + 5:tc + 8, :]
    hlast_ref[0] = h
    xbuf[0:8, :] = xbuf[tc:tc + 8, :]


def _lru_branch(lx, lg, buf8, h0, conv_w, conv_b, wax, bax, lam, tc):
    b, t, c = lx.shape
    seq = lambda: pl.BlockSpec((1, tc, c), lambda i, j: (i, j, 0))
    per_b = lambda r: pl.BlockSpec((1, r, c), lambda i, j: (i, 0, 0))
    const = lambda shp: pl.BlockSpec(shp, lambda i, j: (0, 0))
    return pl.pallas_call(
        functools.partial(_lru_kernel, tc=tc, d_lru=c),
        grid=(b, t // tc),
        in_specs=[seq(), seq(), per_b(8), per_b(1), const((LRU_CONV_W, c)), const((1, c)),
                  const((c, 2 * c)), const((1, 2 * c)), const((1, c))],
        out_specs=[seq(), per_b(LRU_CONV_W - 1), per_b(1)],
        out_shape=[jax.ShapeDtypeStruct((b, t, c), BF16),
                   jax.ShapeDtypeStruct((b, LRU_CONV_W - 1, c), F32),
                   jax.ShapeDtypeStruct((b, 1, c), F32)],
        scratch_shapes=[pltpu.VMEM((tc + 8, c), F32), pltpu.VMEM((1, c), F32),
                        pltpu.VMEM((tc, c), F32), pltpu.VMEM((tc, c), F32)],
        compiler_params=_params(("parallel", "arbitrary")),
        name="rglru_branch",
    )(lx, lg, buf8, h0, conv_w, conv_b.reshape(1, c), wax, bax.reshape(1, 2 * c), lam.reshape(1, c))


def _t5_bucket(dist):
    n = jnp.maximum(dist, 0)
    max_exact = N_BUCKETS // 2
    nf = jnp.maximum(n, 1).astype(F32)
    large = max_exact + (jnp.log(nf / max_exact) / math.log(MAX_DISTANCE / max_exact)
                         * (N_BUCKETS - max_exact)).astype(I32)
    large = jnp.minimum(large, N_BUCKETS - 1)
    return jnp.where(n < max_exact, n, large)


def _bias_lookup(rb_ref, bucket, head):
    acc = jnp.zeros(bucket.shape, F32)
    for b in range(N_BUCKETS):
        acc = jnp.where(bucket == b, rb_ref[b * N_HEADS_A + head], acc)
    return acc


def _bias_tiles_kernel(rb_ref, o_ref, *, tk, tq, d0_min):
    kind = pl.program_id(0)
    head = pl.program_id(1)
    ss = lax.broadcasted_iota(I32, (tk, tq), 0)
    tt = lax.broadcasted_iota(I32, (tk, tq), 1)
    d = d0_min + kind * tk + tt - ss
    val = _bias_lookup(rb_ref, _t5_bucket(d), head)
    o_ref[0, 0] = jnp.where(d < 0, NEG, val * LOG2E)


def _prompt_bias_tiles(rel_bias, tk, tq):
    d0_min = tk - tq
    n_near = -(-(tk - 1 + FAR_DISTANCE - d0_min) // tk)
    tiles = pl.pallas_call(
        functools.partial(_bias_tiles_kernel, tk=tk, tq=tq, d0_min=d0_min),
        grid=(n_near, N_HEADS_A),
        in_specs=[pl.BlockSpec(memory_space=pltpu.SMEM)],
        out_specs=pl.BlockSpec((1, 1, tk, tq), lambda k, h: (k, h, 0, 0)),
        out_shape=jax.ShapeDtypeStruct((n_near, N_HEADS_A, tk, tq), F32),
        compiler_params=_params(("parallel", "parallel")),
        name="prompt_bias_tiles",
    )(rel_bias.reshape(-1))
    return tiles, n_near, d0_min


def _sample_bias_kernel(rb_ref, o_ref, *, t_new):
    kind = pl.program_id(0)
    rows = N_HEADS_A * t_new
    r = lax.broadcasted_iota(I32, (rows, PAGE_SIZE), 0)
    ss = lax.broadcasted_iota(I32, (rows, PAGE_SIZE), 1)
    tq = r % t_new
    head = r // t_new
    base = jnp.where(kind == 0, FAR_DISTANCE + PAGE_SIZE, jnp.where(kind == 1, PAGE_SIZE, 0))
    d = base + tq - ss
    bucket = _t5_bucket(d)
    acc = jnp.zeros((rows, PAGE_SIZE), F32)
    for h in range(N_HEADS_A):
        acc = jnp.where(head == h, _bias_lookup(rb_ref, bucket, h), acc)
    o_ref[0] = jnp.where(d < 0, NEG, acc * LOG2E)


def _sample_bias_tiles(rel_bias, t_new):
    rows = N_HEADS_A * t_new
    return pl.pallas_call(
        functools.partial(_sample_bias_kernel, t_new=t_new),
        grid=(3,),
        in_specs=[pl.BlockSpec(memory_space=pltpu.SMEM)],
        out_specs=pl.BlockSpec((1, rows, PAGE_SIZE), lambda k: (k, 0, 0)),
        out_shape=jax.ShapeDtypeStruct((3, rows, PAGE_SIZE), F32),
        compiler_params=_params(("parallel",)),
        name="sample_bias_tiles",
    )(rel_bias.reshape(-1))


def _tri_tables(n_q_blocks, tq, tk):
    ii, jj = [], []
    for i in range(n_q_blocks):
        for j in range((i + 1) * tq // tk):
            ii.append(i)
            jj.append(j)
    return np.asarray(ii, np.int32), np.asarray(jj, np.int32)


def _scores_kernel(ii_ref, jj_ref, ki_ref, qit_ref, wit_ref, o_ref, *, tk, tq):
    s = pl.program_id(1)
    i = ii_ref[s]
    j = jj_ref[s]
    ki = ki_ref[0]
    acc = jnp.zeros((tk, tq), F32)
    for h in range(N_IDX_HEADS):
        d = jnp.dot(ki, qit_ref[0, h * D_IDX:(h + 1) * D_IDX, :], preferred_element_type=F32)
        acc = acc + jnp.maximum(d, 0.0) * wit_ref[0, h:h + 1, :]
    sc = acc * INDEX_SCALE
    kpos = j * tk + lax.broadcasted_iota(I32, (tk, tq), 0)
    qpos = i * tq + lax.broadcasted_iota(I32, (tk, tq), 1)
    o_ref[0] = jnp.where(kpos <= qpos, sc, -jnp.inf)


def _prompt_scores(kib, qit, wit, tk, tq):
    b, s, _ = kib.shape
    ii, jj = _tri_tables(s // tq, tq, tk)
    grid_spec = pltpu.PrefetchScalarGridSpec(
        num_scalar_prefetch=2,
        grid=(b, len(ii)),
        in_specs=[pl.BlockSpec((1, tk, D_IDX), lambda bb, st, ii, jj: (bb, jj[st], 0)),
                  pl.BlockSpec((1, D_QI, tq), lambda bb, st, ii, jj: (bb, 0, ii[st])),
                  pl.BlockSpec((1, N_IDX_HEADS, tq), lambda bb, st, ii, jj: (bb, 0, ii[st]))],
        out_specs=pl.BlockSpec((1, tk, tq), lambda bb, st, ii, jj: (bb, jj[st], ii[st])),
    )
    return pl.pallas_call(
        functools.partial(_scores_kernel, tk=tk, tq=tq),
        grid_spec=grid_spec,
        out_shape=jax.ShapeDtypeStruct((b, s, s), F32),
        compiler_params=_params(("parallel", "arbitrary")),
        name="prompt_indexer_scores",
    )(jnp.asarray(ii), jnp.asarray(jj), kib, qit, wit)


def _to_key(x):
    bits = lax.bitcast_convert_type(x, I32)
    return jnp.where(bits < 0, -(bits & INT_MAX), bits)


def _from_key(key):
    bits = jnp.where(key < 0, (-key) | INT_MIN, key)
    return lax.bitcast_convert_type(bits, F32)


INT_MIN = -2 ** 31
INT_MAX = 2 ** 31 - 1
NEG_INF_KEY = -0x7F800000
COUNT_ROWS = 32


MIN_NORMAL_KEY = 0x00800000


def _skip_subnormal(m, down):
    if down:
        return jnp.where((m > 0) & (m < MIN_NORMAL_KEY), 0, jnp.where((m < 0) & (m > -MIN_NORMAL_KEY),
                                                                      -MIN_NORMAL_KEY, m))
    return jnp.where((m > 0) & (m < MIN_NORMAL_KEY), MIN_NORMAL_KEY,
                     jnp.where((m < 0) & (m > -MIN_NORMAL_KEY), 0, m))


def _bisect_kth(count_ge, lo, hi, cnt_lo, cnt_hi, k_sel, unit=1, max_in=2, max_passes=None):
    coarse = unit > 1

    def active(lo, hi, cnt_lo, cnt_hi):
        open_ = lo + unit != hi
        if coarse:
            open_ = open_ & ~((lo == 0) & (hi == MIN_NORMAL_KEY)) & ~((lo == -MIN_NORMAL_KEY) & (hi == 0))
        return ((cnt_lo - cnt_hi) > max_in) & open_

    def n_active(*c):
        return jnp.max(active(*c).astype(I32))

    def body(c):
        lo, hi, cnt_lo, cnt_hi = c[:4]
        act = active(lo, hi, cnt_lo, cnt_hi)
        mid = _to_key(0.5 * _from_key(lo) + 0.5 * _from_key(hi)) & -unit
        avg = ((lo >> 1) + (hi >> 1) + (lo & hi & 1)) & -unit
        if coarse:
            mid = jnp.where((hi >> 1) - (lo >> 1) > (1 << 24), avg, mid)
            mid = _skip_subnormal(mid, down=False)
            avg = jnp.where(lo == 0, _skip_subnormal(avg, down=False), _skip_subnormal(avg, down=True))
            mid = jnp.where((lo == 0) & (hi > MIN_NORMAL_KEY), MIN_NORMAL_KEY, mid)
        else:
            mid = jnp.where((lo == 0) & (hi > 1), 1, mid)
        mid = jnp.where((lo < 0) & (hi > 0), 0, mid)
        inside = (mid > lo) & (mid < hi)
        mid = jnp.where(inside, mid, avg)
        cnt = count_ge(mid)
        up = act & (cnt >= k_sel)
        dn = act & (cnt < k_sel)
        lo = jnp.where(up, mid, lo)
        cnt_lo = jnp.where(up, cnt, cnt_lo)
        hi = jnp.where(dn, mid, hi)
        cnt_hi = jnp.where(dn, cnt, cnt_hi)
        return lo, hi, cnt_lo, cnt_hi, n_active(lo, hi, cnt_lo, cnt_hi), c[5] + 1

    keep_going = lambda c: c[4] > 0 if max_passes is None else (c[4] > 0) & (c[5] < max_passes)
    out = lax.while_loop(keep_going, body,
                         (lo, hi, cnt_lo, cnt_hi, n_active(lo, hi, cnt_lo, cnt_hi), jnp.int32(0)))
    return out[:4]


COARSE = 1 << 16
COARSE_PASSES = 8


def _threshold_kernel(ii_ref, jj_ref, nj_ref, st_ref, o_ref, slab, slab16, gmax, *, tk, tq, k_sel, rows,
                      rows16, pos_bits):
    s = pl.program_id(1)
    j = jj_ref[s]
    nj = nj_ref[s]
    key = _to_key(st_ref[0])
    slab[j] = key
    slab16[j] = _from_key(key & -COARSE).astype(BF16)

    @pl.when(j == 0)
    def _():
        gmax[...] = key

    @pl.when(j > 0)
    def _():
        gmax[...] = jnp.maximum(gmax[...], key)

    def reduce_rows(read, n_blocks, init, step, final):
        def body(jb, acc):
            for rc in range(tk // rows):
                acc = step(acc, read(jb, rc), jb, rc)
            return acc
        return final(lax.fori_loop(0, n_blocks, body, jnp.full((rows, tq), init, I32)))

    sum_rows = lambda a: jnp.sum(a, axis=0, keepdims=True)
    read_slab = lambda jb, rc: slab[jb, rc * rows:(rc + 1) * rows, :]
    read_gmax = lambda jb, rc: gmax[rc * rows:(rc + 1) * rows, :]

    def count(read, n_blocks, pred):
        return reduce_rows(read, n_blocks, 0, lambda acc, blk, jb, rc: acc + jnp.where(pred(blk, jb, rc), 1, 0),
                           sum_rows)

    def count16(m):
        m16 = _from_key(m).astype(BF16)
        one = jnp.ones((rows16, tq), BF16)

        def body(jb, acc):
            for rc in range(tk // rows16):
                blk = slab16[jb, rc * rows16:(rc + 1) * rows16, :]
                acc = acc + jnp.where(blk >= m16, one, jnp.zeros_like(one))
            return acc

        acc = lax.fori_loop(0, nj, body, jnp.zeros((rows16, tq), BF16))
        return jnp.sum(acc.astype(F32), axis=0, keepdims=True).astype(I32)

    @pl.when(j == nj - 1)
    def _():
        g = gmax[...]
        top = jnp.max(g, axis=0, keepdims=True) + 1
        zero = jnp.zeros((1, tq), I32)
        if tk >= k_sel:
            g_lo = jnp.min(g, axis=0, keepdims=True)
            lo, _, _, _ = _bisect_kth(lambda m: count(read_gmax, 1, lambda blk, jb, rc: blk >= m),
                                      g_lo, top, jnp.full((1, tq), tk, I32), zero, k_sel, max_in=16)
        else:
            lo = jnp.full((1, tq), NEG_INF_KEY, I32)
        lo = _skip_subnormal(lo & -COARSE, down=True)
        top = _skip_subnormal((top + (COARSE - 1)) & -COARSE, down=False)
        lo, hi, cnt_lo, cnt_hi = _bisect_kth(count16, lo, top, count16(lo), zero, k_sel, unit=COARSE,
                                             max_passes=COARSE_PASSES)
        count_ge = lambda m: count(read_slab, nj, lambda blk, jb, rc: blk >= m)
        lo, hi, cnt_lo, cnt_hi = _bisect_kth(count_ge, lo, hi, cnt_lo, cnt_hi, k_sel)
        mn = reduce_rows(read_slab, nj, INT_MAX,
                         lambda acc, blk, jb, rc: jnp.minimum(acc, jnp.where(blk >= lo, blk, INT_MAX)),
                         lambda a: jnp.min(a, axis=0, keepdims=True))
        mx = reduce_rows(read_slab, nj, INT_MIN,
                         lambda acc, blk, jb, rc: jnp.maximum(acc, jnp.where(blk < hi, blk, INT_MIN)),
                         lambda a: jnp.max(a, axis=0, keepdims=True))
        unit = lo + 1 == hi
        n_in = cnt_lo - cnt_hi
        rank = k_sel - cnt_hi
        kth = jnp.where(unit, lo, jnp.where(rank == 1, mx, mn))
        n_gt = jnp.where(unit | (rank == 1) | (mx == mn), cnt_hi, cnt_hi + 1)
        n_eq = jnp.where(unit, n_in, jnp.where((n_in == 2) & (mx == mn), 2, 1))
        need = k_sel - n_gt
        neg_inf_key = _to_key(jnp.full((1, tq), -jnp.inf, F32))
        tie_cut = (n_eq > need) & (kth != neg_inf_key)
        o_ref[0, 0:1, :] = _from_key(kth)
        o_ref[0, 1:2, :] = jnp.full((1, tq), NO_TIE_CUT, F32)
        o_ref[0, 2:8, :] = jnp.zeros((6, tq), F32)

        @pl.when(jnp.max(tie_cut.astype(I32)) > 0)
        def _():
            row = lax.broadcasted_iota(I32, (rows, tq), 0)

            def pos_step(it, cut):
                cand = cut | (jnp.int32(1) << (pos_bits - 1 - it))
                c = count(read_slab, nj,
                          lambda blk, jb, rc: (blk == kth) & (jb * tk + rc * rows + row < cand))
                return jnp.where(c < need, cand, cut)

            cut = lax.fori_loop(0, pos_bits, pos_step, jnp.zeros((1, tq), I32))
            o_ref[0, 1:2, :] = jnp.where(tie_cut, cut.astype(F32), NO_TIE_CUT)


def _topk_threshold(st, tk, tq, k_sel, causal):
    b, l, n = st.shape
    if causal:
        ii, jj = _tri_tables(n // tq, tq, tk)
    else:
        nb = l // tk
        ii = np.repeat(np.arange(n // tq, dtype=np.int32), nb)
        jj = np.tile(np.arange(nb, dtype=np.int32), n // tq)
    nj = np.asarray([(i + 1) * tq // tk if causal else l // tk for i in ii], np.int32)
    rows, rows16 = COUNT_ROWS, 2 * COUNT_ROWS
    n_blocks = int(nj.max())
    assert tk % rows16 == 0
    assert n_blocks * (tk // rows16) <= 256
    grid_spec = pltpu.PrefetchScalarGridSpec(
        num_scalar_prefetch=3,
        grid=(b, len(ii)),
        in_specs=[pl.BlockSpec((1, tk, tq), lambda bb, st_, ii, jj, nj: (bb, jj[st_], ii[st_]))],
        out_specs=pl.BlockSpec((1, 8, tq), lambda bb, st_, ii, jj, nj: (bb, 0, ii[st_])),
        scratch_shapes=[pltpu.VMEM((n_blocks, tk, tq), I32), pltpu.VMEM((n_blocks, tk, tq), BF16),
                        pltpu.VMEM((tk, tq), I32)],
    )
    slab_bytes = n_blocks * tk * tq * 6
    return pl.pallas_call(
        functools.partial(_threshold_kernel, tk=tk, tq=tq, k_sel=k_sel, rows=rows, rows16=rows16,
                          pos_bits=max(1, (l - 1).bit_length())),
        grid_spec=grid_spec,
        out_shape=jax.ShapeDtypeStruct((b, 8, n), F32),
        compiler_params=pltpu.CompilerParams(
            dimension_semantics=("parallel", "arbitrary"),
            vmem_limit_bytes=max(VMEM_LIMIT_BYTES, slab_bytes + 10 * 1024 * 1024)),
        name="topk_threshold",
    )(jnp.asarray(ii), jnp.asarray(jj), jnp.asarray(nj), st)


def _selected(score, key_pos, thr, cut):
    return (score > thr) | ((score == thr) & (key_pos <= cut))


ACC_ROWS = HEAD_DIM + 16


def _prompt_attn_kernel(ii_ref, jj_ref, nj_ref, cut_ref, kb_ref, qt_ref, vt_ref, st_ref, sel_ref, bias_ref,
                        far_ref, o_ref, m_s, acc_s, mb_s, lg_s, *, tk, tq, n_near, d0_min, n_q_blocks):
    b = pl.program_id(0)
    s = pl.program_id(1)
    i = ii_ref[s]
    j = jj_ref[s]
    nj = nj_ref[s]

    @pl.when(j == 0)
    def _():
        m_s[...] = jnp.full(m_s.shape, NEG, F32)
        acc_s[...] = jnp.zeros(acc_s.shape, F32)

    thr = sel_ref[0, 0:1, :]
    has_cut = cut_ref[b * n_q_blocks + i] > 0

    @pl.when(jnp.logical_not(has_cut))
    def _():
        mb_s[...] = jnp.where(st_ref[0] >= thr, 0.0, NEG)

    @pl.when(has_cut)
    def _():
        key_pos = (j * tk + lax.broadcasted_iota(I32, (tk, tq), 0)).astype(F32)
        mb_s[...] = jnp.where(_selected(st_ref[0], key_pos, thr, sel_ref[0, 1:2, :]), 0.0, NEG)

    kind = (i * tq - j * tk - d0_min) // tk
    ones = jnp.ones((ACC_ROWS - HEAD_DIM, tk), BF16)

    def qk(h):
        hs = slice(h * HEAD_DIM, (h + 1) * HEAD_DIM)
        lg_s[h % 2] = jnp.dot(kb_ref[0, :, hs], qt_ref[0, hs, :], preferred_element_type=F32)

    def heads(far):
        qk(0)
        for h in range(N_HEADS_A):
            if h + 1 < N_HEADS_A:
                qk(h + 1)
            lg = lg_s[h % 2] + mb_s[...]
            if far:
                shift = far_ref[h] * LOG2E
            else:
                lg = lg + bias_ref[kind, h]
                shift = 0.0
            m_old = m_s[h:h + 1, :]
            m_new = jnp.maximum(m_old, jnp.max(lg, axis=0, keepdims=True) + shift)
            alpha = jnp.exp2(m_old - m_new)
            p = jnp.exp2(lg - (m_new - shift)).astype(BF16)
            v_aug = jnp.concatenate([vt_ref[0, h * HEAD_DIM:(h + 1) * HEAD_DIM, :], ones], axis=0)
            rs = slice(h * ACC_ROWS, (h + 1) * ACC_ROWS)
            acc_s[rs, :] = alpha * acc_s[rs, :] + jnp.dot(v_aug, p, preferred_element_type=F32)
            m_s[h:h + 1, :] = m_new

    pl.when(kind >= n_near)(lambda: heads(True))
    pl.when(kind < n_near)(lambda: heads(False))

    @pl.when(j == nj - 1)
    def _():
        for h in range(N_HEADS_A):
            r0 = h * ACC_ROWS
            o_ref[0, h * HEAD_DIM:(h + 1) * HEAD_DIM, :] = (
                acc_s[r0:r0 + HEAD_DIM, :] / acc_s[r0 + HEAD_DIM:r0 + HEAD_DIM + 1, :]).astype(o_ref.dtype)


def _prompt_attention(kb, qt, vt, st, sel, has_cut, bias_tiles, far_bias, n_near, d0_min, tk, tq):
    b, s, d = kb.shape
    ii, jj = _tri_tables(s // tq, tq, tk)
    nj = np.asarray([(i + 1) * tq // tk for i in ii], np.int32)
    grid_spec = pltpu.PrefetchScalarGridSpec(
        num_scalar_prefetch=4,
        grid=(b, len(ii)),
        in_specs=[pl.BlockSpec((1, tk, d), lambda bb, t, ii, jj, nj, hc: (bb, jj[t], 0)),
                  pl.BlockSpec((1, d, tq), lambda bb, t, ii, jj, nj, hc: (bb, 0, ii[t])),
                  pl.BlockSpec((1, d, tk), lambda bb, t, ii, jj, nj, hc: (bb, 0, jj[t])),
                  pl.BlockSpec((1, tk, tq), lambda bb, t, ii, jj, nj, hc: (bb, jj[t], ii[t])),
                  pl.BlockSpec((1, 8, tq), lambda bb, t, ii, jj, nj, hc: (bb, 0, ii[t])),
                  pl.BlockSpec((n_near, N_HEADS_A, tk, tq), lambda bb, t, ii, jj, nj, hc: (0, 0, 0, 0),
                               pipeline_mode=pl.Buffered(1)),
                  pl.BlockSpec(memory_space=pltpu.SMEM)],
        out_specs=pl.BlockSpec((1, d, tq), lambda bb, t, ii, jj, nj, hc: (bb, 0, ii[t])),
        scratch_shapes=[pltpu.VMEM((N_HEADS_A, tq), F32), pltpu.VMEM((N_HEADS_A * ACC_ROWS, tq), F32),
                        pltpu.VMEM((tk, tq), F32), pltpu.VMEM((2, tk, tq), F32)],
    )
    return pl.pallas_call(
        functools.partial(_prompt_attn_kernel, tk=tk, tq=tq, n_near=n_near, d0_min=d0_min,
                          n_q_blocks=s // tq),
        grid_spec=grid_spec,
        out_shape=jax.ShapeDtypeStruct((b, d, s), BF16),
        compiler_params=_params(("parallel", "arbitrary")),
        name="prompt_sparse_attention",
    )(jnp.asarray(ii), jnp.asarray(jj), jnp.asarray(nj), has_cut, kb, qt, vt, st, sel, bias_tiles, far_bias)


PAGES_PER_STEP = 16


def _page_spec(block, u):
    zeros = (0,) * (len(block) - 1)
    return pl.BlockSpec(block, lambda sq, g, pt: (pt[sq, g * PAGES_PER_STEP + u],) + zeros)


def _head_sum(x, t_new):
    acc = x[0:t_new, :]
    for h in range(1, N_IDX_HEADS):
        acc = acc + x[h * t_new:(h + 1) * t_new, :]
    return acc


def _sample_scores_kernel(pt_ref, *refs, t_new):
    pages = refs[:PAGES_PER_STEP]
    qi_ref, w_ref, kinew_ref, o_ref, onew_ref = refs[PAGES_PER_STEP:]
    qi = qi_ref[0]
    w = w_ref[0]

    def score(d):
        return _head_sum(jnp.maximum(d, 0.0) * w, t_new) * INDEX_SCALE

    for u in range(PAGES_PER_STEP):
        o_ref[0, :, u * PAGE_SIZE:(u + 1) * PAGE_SIZE] = score(
            jnp.dot(qi, pages[u][0].astype(BF16), preferred_element_type=F32))

    sc = score(lax.dot_general(qi, kinew_ref[0], (((1,), (1,)), ((), ())), preferred_element_type=F32))
    tq = lax.broadcasted_iota(I32, (t_new, PAGE_SIZE), 0)
    ss = lax.broadcasted_iota(I32, (t_new, PAGE_SIZE), 1)
    onew_ref[0] = jnp.where(ss <= tq, sc, -jnp.inf)


def _sample_scores(page_table, cache_idx, qi_rows, w_rows, ki_new_pad, t_new):
    bd, n_pages = page_table.shape
    rows = N_IDX_HEADS * t_new
    per_seq = lambda shp: pl.BlockSpec((1,) + shp, lambda sq, g, pt: (sq, 0, 0))
    grid_spec = pltpu.PrefetchScalarGridSpec(
        num_scalar_prefetch=1,
        grid=(bd, n_pages // PAGES_PER_STEP),
        in_specs=[_page_spec((1, D_IDX, PAGE_SIZE), u) for u in range(PAGES_PER_STEP)]
        + [per_seq((rows, D_IDX)), per_seq((rows, 1)), per_seq((PAGE_SIZE, D_IDX))],
        out_specs=[pl.BlockSpec((1, t_new, PAGES_PER_STEP * PAGE_SIZE), lambda sq, g, pt: (sq, 0, g)),
                   per_seq((t_new, PAGE_SIZE))],
    )
    return pl.pallas_call(
        functools.partial(_sample_scores_kernel, t_new=t_new),
        grid_spec=grid_spec,
        out_shape=[jax.ShapeDtypeStruct((bd, t_new, n_pages * PAGE_SIZE), F32),
                   jax.ShapeDtypeStruct((bd, t_new, PAGE_SIZE), F32)],
        compiler_params=_params(("parallel", "arbitrary")),
        name="sample_indexer_scores",
    )(page_table, *([cache_idx] * PAGES_PER_STEP), qi_rows, w_rows, ki_new_pad)


def _sample_attn_kernel(pt_ref, *refs, t_new, n_groups, far_pages):
    kp = refs[:PAGES_PER_STEP]
    vp = refs[PAGES_PER_STEP:2 * PAGES_PER_STEP]
    (q_ref, knew_ref, vnew_ref, sc_ref, scnew_ref, sel_ref, bias_ref, o_ref,
     qb_s, m_s, l_s, acc_s) = refs[2 * PAGES_PER_STEP:]
    g = pl.program_id(1)
    rows = N_HEADS_A * t_new
    d = N_HEADS_A * HEAD_DIM

    @pl.when(g == 0)
    def _():
        q_all = jnp.concatenate([q_ref[0].astype(F32)] * N_HEADS_A, axis=0)
        r = lax.broadcasted_iota(I32, (rows, d), 0)
        c = lax.broadcasted_iota(I32, (rows, d), 1)
        qb_s[...] = jnp.where(r // t_new == c // HEAD_DIM, q_all, 0.0).astype(BF16)
        m_s[...] = jnp.full(m_s.shape, NEG, F32)
        l_s[...] = jnp.zeros(l_s.shape, F32)
        acc_s[...] = jnp.zeros(acc_s.shape, F32)

    thr = sel_ref[0, :, 0:1]
    cut = sel_ref[0, :, 1:2]
    qb = qb_s[...]

    contract_last = (((1,), (1,)), ((), ()))

    def logits(qk, score, key_pos0, bias):
        pos = (key_pos0 + lax.broadcasted_iota(I32, (t_new, PAGE_SIZE), 1)).astype(F32)
        mb = jnp.where(_selected(score, pos, thr, cut), 0.0, NEG)
        return qk + bias + jnp.concatenate([mb] * N_HEADS_A, axis=0)

    def update(lg, values_bf, channel_major):
        m_old = m_s[...]
        m_new = jnp.maximum(m_old, jnp.max(lg, axis=1, keepdims=True))
        alpha = jnp.exp2(m_old - m_new)
        p = jnp.exp2(lg - m_new)
        l_s[...] = alpha * l_s[...] + jnp.sum(p, axis=1, keepdims=True)
        pv = jnp.zeros((rows, d), F32)
        for u, vb in enumerate(values_bf):
            pu = p[:, u * PAGE_SIZE:(u + 1) * PAGE_SIZE].astype(BF16)
            if channel_major:
                pv = pv + lax.dot_general(pu, vb, contract_last, preferred_element_type=F32)
            else:
                pv = pv + jnp.dot(pu, vb, preferred_element_type=F32)
        acc_s[...] = alpha * acc_s[...] + pv
        m_s[...] = m_new

    lgs = []
    for u in range(PAGES_PER_STEP):
        page = g * PAGES_PER_STEP + u
        bias = bias_ref[jnp.where(page < far_pages, 0, 1)]
        qk = jnp.dot(qb, kp[u][0].astype(BF16), preferred_element_type=F32)
        lgs.append(logits(qk, sc_ref[0, :, u * PAGE_SIZE:(u + 1) * PAGE_SIZE], page * PAGE_SIZE, bias))
    update(jnp.concatenate(lgs, axis=1), [vp[u][0].astype(BF16) for u in range(PAGES_PER_STEP)], True)

    @pl.when(g == n_groups - 1)
    def _():
        past = n_groups * PAGES_PER_STEP * PAGE_SIZE
        qk = lax.dot_general(qb, knew_ref[0], contract_last, preferred_element_type=F32)
        update(logits(qk, scnew_ref[0], past, bias_ref[2]), [vnew_ref[0]], False)
        out = []
        for h in range(N_HEADS_A):
            rs = slice(h * t_new, (h + 1) * t_new)
            out.append(acc_s[rs, h * HEAD_DIM:(h + 1) * HEAD_DIM] / l_s[rs, :])
        o_ref[0] = jnp.concatenate(out, axis=1).astype(o_ref.dtype)


def _sample_attention(page_table, cache_k, cache_v, q_s, k_new_pad, v_new_pad, sc_past, sc_new,
                      sel_rows, bias_tiles, t_new):
    bd, n_pages = page_table.shape
    d = cache_k.shape[1]
    rows = N_HEADS_A * t_new
    n_groups = n_pages // PAGES_PER_STEP
    far_pages = n_pages - 1
    assert PAGE_SIZE + 1 >= FAR_DISTANCE
    per_seq = lambda shp: pl.BlockSpec((1,) + shp, lambda sq, g, pt: (sq, 0, 0))
    grid_spec = pltpu.PrefetchScalarGridSpec(
        num_scalar_prefetch=1,
        grid=(bd, n_groups),
        in_specs=[_page_spec((1, d, PAGE_SIZE), u) for u in range(PAGES_PER_STEP)] * 2
        + [per_seq((t_new, d)), per_seq((PAGE_SIZE, d)), per_seq((PAGE_SIZE, d)),
           pl.BlockSpec((1, t_new, PAGES_PER_STEP * PAGE_SIZE), lambda sq, g, pt: (sq, 0, g)),
           per_seq((t_new, PAGE_SIZE)), per_seq((t_new, 128)),
           pl.BlockSpec((3, rows, PAGE_SIZE), lambda sq, g, pt: (0, 0, 0))],
        out_specs=per_seq((t_new, d)),
        scratch_shapes=[pltpu.VMEM((rows, d), BF16), pltpu.VMEM((rows, 1), F32),
                        pltpu.VMEM((rows, 1), F32), pltpu.VMEM((rows, d), F32)],
    )
    return pl.pallas_call(
        functools.partial(_sample_attn_kernel, t_new=t_new, n_groups=n_groups, far_pages=far_pages),
        grid_spec=grid_spec,
        out_shape=jax.ShapeDtypeStruct((bd, t_new, d), BF16),
        compiler_params=_params(("parallel", "arbitrary")),
        name="sample_sparse_attention",
    )(page_table, *([cache_k] * PAGES_PER_STEP), *([cache_v] * PAGES_PER_STEP), q_s, k_new_pad,
      v_new_pad, sc_past, sc_new, sel_rows, bias_tiles)


def _ffn_kernel(*refs, tm, d_ff, tiles_per_seq, fixup):
    if fixup:
        (x_ref, a_ref, r_ref, wo_ref, gf_ref, wup_ref, cw_ref, cb_ref, wdn_ref, gl_ref, x0_ref, x1_ref,
         y_ref, g_ref, gbuf) = refs
    else:
        (x_ref, a_ref, r_ref, wo_ref, gf_ref, wup_ref, cw_ref, cb_ref, wdn_ref, gl_ref,
         y_ref, g_ref, gbuf) = refs
    i = pl.program_id(0)
    d_attn = a_ref.shape[1]
    x1 = (x_ref[...] + jnp.dot(a_ref[...], wo_ref[0:d_attn, :], preferred_element_type=F32)
          + jnp.dot(r_ref[...], wo_ref[d_attn:, :], preferred_element_type=F32))
    xn = _rmsnorm(x1, gf_ref[...]).astype(BF16)
    up = jnp.dot(xn, wup_ref[...], preferred_element_type=F32)
    g = up[:, :d_ff]
    u = up[:, d_ff:]

    @pl.when(i % tiles_per_seq == 0)
    def _():
        gbuf[0:8, :] = jnp.zeros((8, d_ff), F32)

    gbuf[8:8 + tm, :] = g
    prev2 = gbuf[6:6 + tm, :]
    prev1 = gbuf[7:7 + tm, :]
    if fixup:
        t = lax.broadcasted_iota(I32, (tm, d_ff), 0) % 8
        prev2 = jnp.where(t < 2, x0_ref[...], prev2)
        prev1 = jnp.where(t < 1, x1_ref[...], prev1)
    gc = prev2 * cw_ref[0:1, :] + cb_ref[...]
    gc = gc + prev1 * cw_ref[1:2, :]
    gc = gc + g * cw_ref[2:3, :]
    hid = (jax.nn.gelu(gc) * u).astype(BF16)
    x2 = x1 + jnp.dot(hid, wdn_ref[...], preferred_element_type=F32)
    y_ref[...] = _rmsnorm(x2, gl_ref[...])
    if fixup:
        g_ref[...] = g
    else:
        g_ref[0] = g[tm - (FFN_CONV_W - 1):, :]
    gbuf[0:8, :] = gbuf[tm:tm + 8, :]


def _merge_and_ffn(x2d, attn, lru, w_out, norm_ffn, w_up, conv_w, conv_b, w_down, norm_final, tm,
                   rows_per_seq, fix0=None, fix1=None):
    n, d = x2d.shape
    d_ff = w_down.shape[0]
    fixup = fix0 is not None
    tiles_per_seq = max(rows_per_seq // tm, 1)
    row = lambda w: pl.BlockSpec((tm, w), lambda i: (i, 0))
    const = lambda a: pl.BlockSpec(a.shape, lambda i: (0,) * a.ndim, pipeline_mode=pl.Buffered(1))
    cb2 = conv_b.reshape(1, d_ff)
    gf2 = norm_ffn.reshape(1, d)
    gl2 = norm_final.reshape(1, d)
    ins = [x2d, attn, lru, w_out, gf2, w_up, conv_w, cb2, w_down, gl2]
    in_specs = [row(d), row(attn.shape[1]), row(lru.shape[1]), const(w_out), const(gf2), const(w_up),
                const(conv_w), const(cb2), const(w_down), const(gl2)]
    if fixup:
        ins += [fix0, fix1]
        in_specs += [row(d_ff), row(d_ff)]
        g_spec = row(d_ff)
        g_shape = jax.ShapeDtypeStruct((n, d_ff), F32)
    else:
        g_spec = pl.BlockSpec((1, FFN_CONV_W - 1, d_ff), lambda i: (i // tiles_per_seq, 0, 0))
        g_shape = jax.ShapeDtypeStruct((n // rows_per_seq, FFN_CONV_W - 1, d_ff), F32)
    return pl.pallas_call(
        functools.partial(_ffn_kernel, tm=tm, d_ff=d_ff, tiles_per_seq=tiles_per_seq, fixup=fixup),
        grid=(n // tm,),
        in_specs=in_specs,
        out_specs=[row(d), g_spec],
        out_shape=[jax.ShapeDtypeStruct((n, d), F32), g_shape],
        scratch_shapes=[pltpu.VMEM((tm + 8, d_ff), F32)],
        compiler_params=_params(("arbitrary",)),
        name="merge_and_conv_ffn",
    )(*ins)


PROMPT_TILE = 512


def _key_tile(n_keys, unit=64, largest=1024):
    return max(t for t in range(unit, largest + 1, unit) if n_keys % t == 0)


def _block_diag(w):
    h, a, b = w.shape
    eye = jnp.eye(h, dtype=w.dtype)
    return (w[:, :, None, :] * eye[:, None, :, None]).reshape(h * a, h * b)


def kernel(x_prompt, x_sample, cache_k, cache_v, cache_idx_k, state_lru_h, state_lru_conv, state_ffn_conv, page_table, w_in, w_out, norm_mix, norm_ffn, norm_final, rel_bias, lru_conv_w, lru_conv_b, lru_w_a, lru_b_a, lru_w_x, lru_b_x, lru_lambda, ffn_w_up, ffn_conv_w, ffn_conv_b, ffn_w_down):
    bp, s, d = x_prompt.shape
    bd, t_new, _ = x_sample.shape
    depth = w_in.shape[0]
    assert depth == 1 and t_new == 8
    d_lru = lru_lambda.shape[1]
    d_ff = ffn_w_down.shape[1]
    n_pages = page_table.shape[1]
    past = n_pages * PAGE_SIZE
    l = 0

    c_qi_end = 3 * D_ATTN + D_QI
    c_wi_end = c_qi_end + D_IDX + N_IDX_HEADS
    pad_cols = (-(D_IDX + N_IDX_HEADS)) % 128
    w_re = jnp.concatenate([w_in[l][:, :c_qi_end], w_in[l][:, c_wi_end:], w_in[l][:, c_qi_end:c_wi_end],
                            jnp.zeros((d, pad_cols), w_in.dtype)], axis=1).astype(BF16)
    wax = jnp.concatenate([_block_diag(lru_w_a[l]), _block_diag(lru_w_x[l])], axis=1).astype(BF16)
    bax = jnp.concatenate([lru_b_a[l], lru_b_x[l]])
    w_out_b = w_out[l].astype(BF16)
    w_up_b = ffn_w_up[l].astype(BF16)
    w_dn_b = ffn_w_down[l].astype(BF16)

    tk = tq = PROMPT_TILE
    (qb, k, v, kb, vb, qib, ki, kib, wi, lx, lg) = _in_projection(
        x_prompt.reshape(bp * s, d), norm_mix[l], w_re, d_lru, tm=512)
    r3 = lambda a: a.reshape(bp, s, a.shape[-1])
    tr = lambda a: jnp.swapaxes(r3(a), 1, 2)
    st = _prompt_scores(r3(kib), tr(qib), tr(wi), tk, tq)
    sel = _topk_threshold(st, tk, tq, min(TOPK_MAX, s // 4), causal=True)
    has_cut = jnp.any((sel[:, 1, :] < NO_TIE_CUT).reshape(bp * (s // tq), tq), axis=1).astype(I32)
    bias_tiles, n_near, d0_min = _prompt_bias_tiles(rel_bias, tk, tq)
    attn_t = _prompt_attention(r3(kb), tr(qb), tr(vb), st, sel, has_cut, bias_tiles,
                               rel_bias[N_BUCKETS - 1], n_near, d0_min, tk, tq)
    attn_p = jnp.swapaxes(attn_t, 1, 2).reshape(bp * s, D_ATTN)
    lru_p, lconv_p, hlast_p = _lru_branch(
        r3(lx), r3(lg), jnp.zeros((bp, 8, d_lru), F32), jnp.zeros((bp, 1, d_lru), F32),
        lru_conv_w[l], lru_conv_b[l], wax, bax, lru_lambda[l], tc=512)
    y_p, fconv_p = _merge_and_ffn(
        x_prompt.reshape(bp * s, d), attn_p, lru_p.reshape(bp * s, d_lru), w_out_b, norm_ffn[l], w_up_b,
        ffn_conv_w[l], ffn_conv_b[l], w_dn_b, norm_final, tm=256, rows_per_seq=s)
    k_p, v_p, ki_p = k, v, ki

    n_s = bd * t_new
    (qb, k, v, kb, vb, qib, ki, kib, wi, lx, lg) = _in_projection(
        x_sample.reshape(n_s, d), norm_mix[l], w_re, d_lru, tm=min(512, n_s))
    s3 = lambda a: a.reshape(bd, t_new, a.shape[-1])
    qi_rows = s3(qib).reshape(bd, t_new, N_IDX_HEADS, D_IDX).transpose(0, 2, 1, 3).reshape(
        bd, N_IDX_HEADS * t_new, D_IDX)
    w_rows = s3(wi).transpose(0, 2, 1).reshape(bd, N_IDX_HEADS * t_new, 1)
    pad_new = lambda a: jnp.pad(s3(a), ((0, 0), (0, PAGE_SIZE - t_new), (0, 0)))
    idx_pages = jnp.swapaxes(cache_idx_k[l], 1, 2)
    k_pages = cache_k[l].transpose(0, 2, 3, 1).reshape(-1, D_ATTN, PAGE_SIZE)
    v_pages = cache_v[l].transpose(0, 2, 3, 1).reshape(-1, D_ATTN, PAGE_SIZE)
    sc_past, sc_new = _sample_scores(page_table, idx_pages, qi_rows, w_rows, pad_new(kib), t_new)
    l_keys = past + PAGE_SIZE
    sc_all = jnp.concatenate([sc_past, sc_new], axis=2).reshape(n_s, l_keys)
    sel_s = _topk_threshold(sc_all.T[None], _key_tile(l_keys), min(256, n_s),
                            min(TOPK_MAX, (past + t_new) // 4), causal=False)
    sel_rows = jnp.pad(sel_s[0].T.reshape(bd, t_new, 8), ((0, 0), (0, 0), (0, 120)))
    attn_s = _sample_attention(
        page_table, k_pages, v_pages, s3(qb), pad_new(kb), pad_new(vb), sc_past, sc_new, sel_rows,
        _sample_bias_tiles(rel_bias, t_new), t_new)
    buf8 = jnp.pad(state_lru_conv[l], ((0, 0), (8 - (LRU_CONV_W - 1), 0), (0, 0)))
    lru_s, lconv_s, hlast_s = _lru_branch(
        s3(lx), s3(lg), buf8, state_lru_h[l][:, None, :], lru_conv_w[l], lru_conv_b[l], wax, bax,
        lru_lambda[l], tc=t_new)
    fbuf = state_ffn_conv[l]
    fix0 = jnp.pad(fbuf, ((0, 0), (0, t_new - 2), (0, 0))).reshape(n_s, d_ff)
    fix1 = jnp.pad(fbuf[:, 1:2], ((0, 0), (0, t_new - 1), (0, 0))).reshape(n_s, d_ff)
    y_s, g_s = _merge_and_ffn(
        x_sample.reshape(n_s, d), attn_s.reshape(n_s, D_ATTN), lru_s.reshape(n_s, d_lru), w_out_b,
        norm_ffn[l], w_up_b, ffn_conv_w[l], ffn_conv_b[l], w_dn_b, norm_final, tm=min(256, n_s),
        rows_per_seq=n_s, fix0=fix0, fix1=fix1)
    fconv_s = g_s.reshape(bd, t_new, d_ff)[:, t_new - (FFN_CONV_W - 1):, :]

    heads = lambda a, b_, t_: a.reshape(1, b_, t_, N_HEADS_A, HEAD_DIM)
    return (y_p.reshape(bp, s, d), y_s.reshape(bd, t_new, d),
            heads(k_p, bp, s), heads(v_p, bp, s), ki_p.reshape(1, bp, s, D_IDX),
            hlast_p.reshape(1, bp, d_lru), lconv_p[None], fconv_p[None],
            heads(k, bd, t_new), heads(v, bd, t_new), ki.reshape(1, bd, t_new, D_IDX),
            hlast_s.reshape(1, bd, d_lru), lconv_s[None], fconv_s[None])
```
